```python
import functools
import jax, jax.numpy as jnp
from jax import lax
import numpy as np

D_MODEL = 1024
BATCH = 8
SEQ = 2048
DEPTH = 1
DEC_BATCH = 16
DEC_SEQ = 32
PAST_LEN = 4096

CHUNK = 64
D_MIX = D_MODEL
RET_HEADS = 4
RET_DK = 128
RET_DV = 128
SB_HEADS = 4
SB_DH = 128
SB_BLOCK = 128
ROPE_BASE = 10000.0
NORM_EPS = 1e-6
RET_W = RET_HEADS * RET_DK
RET_WV = RET_HEADS * RET_DV
SB_W = SB_HEADS * SB_DH
IN_COLS = 2 * RET_W + 2 * RET_WV + 3 * SB_W
PEER_HEADS = 8
PEER_NKEYS = 128
PEER_EXPERTS = PEER_NKEYS * PEER_NKEYS
PEER_DKEY = 256
PEER_TOPK = 16
PEER_TOKBLOCK = 256

kernel_name = 'hymba_retention_stickbreaking_peer_stream'


def rmsnorm(x, g):
    xf = x.astype(jnp.float32)
    y = xf * lax.rsqrt(jnp.mean(xf * xf, axis=-1, keepdims=True) + NORM_EPS)
    return (y * g.astype(jnp.float32)).astype(x.dtype)


def adaln(c, w_ada, b_ada):
    mod = jax.nn.silu(c) @ w_ada + b_ada
    return jnp.split(mod[:, None, :], 6, axis=-1)


def modulate(x, g, shift, scale):
    return rmsnorm(x, g) * (1 + scale) + shift


def rope(x, pos):
    half = x.shape[-1] // 2
    inv_freq = ROPE_BASE ** (-jnp.arange(half, dtype=jnp.float32) / half)
    ang = pos.astype(jnp.float32)[:, None] * inv_freq[None, :]
    cos = jnp.cos(ang)[None, :, None, :].astype(x.dtype)
    sin = jnp.sin(ang)[None, :, None, :].astype(x.dtype)
    x1, x2 = x[..., :half], x[..., half:]
    return jnp.concatenate([x1 * cos - x2 * sin, x1 * sin + x2 * cos], axis=-1)


def retention_log_decay():
    return jnp.log1p(-jnp.exp2(-5.0 - jnp.arange(RET_HEADS, dtype=jnp.float32)))


def split_heads(proj):
    B, L = proj.shape[:2]
    c0 = RET_W
    c1 = 2 * RET_W
    c2 = c1 + RET_WV
    c3 = c2 + RET_WV
    c4 = c3 + SB_W
    c5 = c4 + SB_W
    rq, rk, rv, rg, sq, sk, sv = jnp.split(proj, [c0, c1, c2, c3, c4, c5], axis=-1)
    return (rq.reshape(B, L, RET_HEADS, RET_DK), rk.reshape(B, L, RET_HEADS, RET_DK),
            rv.reshape(B, L, RET_HEADS, RET_DV), rg,
            sq.reshape(B, L, SB_HEADS, SB_DH), sk.reshape(B, L, SB_HEADS, SB_DH),
            sv.reshape(B, L, SB_HEADS, SB_DH))


def retention_chunk(S, q, k, v):
    L = q.shape[1]
    log_g = retention_log_decay()
    idx = jnp.arange(L, dtype=jnp.float32)
    diff = idx[:, None] - idx[None, :]
    causal = diff >= 0
    intra = jnp.where(causal[None], jnp.exp(jnp.where(causal, diff, 0.0)[None] * log_g[:, None, None]), 0.0)
    scores = jnp.einsum('bihd,bjhd->bhij', q, k) * intra.astype(q.dtype)[None]
    o = jnp.einsum('bhij,bjhe->bihe', scores, v)
    q_decay = jnp.exp((idx[:, None] + 1.0) * log_g[None, :]).astype(q.dtype)
    o = o + jnp.einsum('bihd,bhde->bihe', q, S) * q_decay[None, :, :, None]
    k_decay = jnp.exp((L - 1.0 - idx)[:, None] * log_g[None, :]).astype(k.dtype)
    S_new = (jnp.exp(L * log_g)[None, :, None, None].astype(S.dtype) * S
             + jnp.einsum('bjhd,bjhe->bhde', k * k_decay[None, :, :, None], v))
    return o, S_new


def retention_prompt(q, k, v):
    B, L, H, dk = q.shape
    dv = v.shape[-1]
    nc = L // CHUNK

    def to_chunks(t):
        return t.reshape(B, nc, CHUNK, H, t.shape[-1]).transpose(1, 0, 2, 3, 4)

    def step(S, qkv):
        qc, kc, vc = qkv
        o, S = retention_chunk(S, qc, kc, vc)
        return S, o

    S0 = jnp.zeros((B, H, dk, dv), q.dtype)
    S, o = lax.scan(step, S0, (to_chunks(q), to_chunks(k), to_chunks(v)))
    o = o.transpose(1, 0, 2, 3, 4).reshape(B, L, H, dv)
    return o, S


def stick_breaking_block(q, k, v, q_pos, k_pos):
    z = jnp.einsum('bqhd,bkhd->bhqk', q, k).astype(jnp.float32) * (SB_DH ** -0.5)
    mask = (k_pos[None, :] < q_pos[:, None])[None, None]
    log_beta = jax.nn.log_sigmoid(z)
    log_keep = jnp.where(mask, jax.nn.log_sigmoid(-z), 0.0)
    after = lax.cumsum(log_keep, axis=3, reverse=True) - log_keep
    A = jnp.where(mask, jnp.exp(log_beta + after), 0.0)
    return jnp.einsum('bhqk,bkhd->bqhd', A.astype(v.dtype), v)


def stick_breaking_prompt(q, k, v):
    L = q.shape[1]
    outs = []
    for i in range(L // SB_BLOCK):
        s, e = i * SB_BLOCK, (i + 1) * SB_BLOCK
        outs.append(stick_breaking_block(q[:, s:e], k[:, :e], v[:, :e], jnp.arange(s, e), jnp.arange(e)))
    return jnp.concatenate(outs, axis=1)


def mix_first_chunk(rq, rk, rv, sq, sk, sv):
    ro, s_ret = retention_prompt(rq, rk, rv)
    so = stick_breaking_prompt(sq, sk, sv)
    return ro, so, (s_ret, sk, sv)


def mix_later_chunk(s_prev, k_past, v_past, rq, rk, rv, sq, sk, sv):
    P = k_past.shape[1]
    Ls = sq.shape[1]
    ro, s_new = retention_chunk(s_prev, rq, rk, rv)
    k_all = jnp.concatenate([k_past, sk], axis=1)
    v_all = jnp.concatenate([v_past, sv], axis=1)
    so = stick_breaking_block(sq, k_all, v_all, P + jnp.arange(Ls), jnp.arange(P + Ls))
    return ro, so, (s_new, sk, sv)


def merge_groups(ro, rg, so, ret_norm_g, w_out):
    B, L = ro.shape[:2]
    of = ro.astype(jnp.float32)
    mu = jnp.mean(of, axis=-1, keepdims=True)
    cen = of - mu
    on = (cen * lax.rsqrt(jnp.mean(cen * cen, axis=-1, keepdims=True) + NORM_EPS)).astype(ro.dtype)
    ret = on.reshape(B, L, RET_WV) * ret_norm_g * jax.nn.silu(rg)
    return jnp.concatenate([ret, so.reshape(B, L, SB_W)], axis=-1) @ w_out


def peer_ffn(x, w_query, sub_keys, expert_u, expert_v):
    n = x.shape[0]
    pad = (-n) % PEER_TOKBLOCK
    blocks = jnp.pad(x, ((0, pad), (0, 0))).reshape(-1, PEER_TOKBLOCK, D_MODEL)

    def one(xb):
        T = xb.shape[0]
        q = (xb @ w_query).reshape(T, PEER_HEADS, 2, PEER_DKEY // 2)
        s = jnp.einsum('thpd,phkd->thpk', q, sub_keys).astype(jnp.float32)
        sv_, si = lax.top_k(s, PEER_TOPK)
        cand = (sv_[:, :, 0, :, None] + sv_[:, :, 1, None, :]).reshape(T, PEER_HEADS, PEER_TOPK * PEER_TOPK)
        cand_idx = (si[:, :, 0, :, None] * PEER_NKEYS + si[:, :, 1, None, :]).reshape(T, PEER_HEADS, PEER_TOPK * PEER_TOPK)
        top_s, pos = lax.top_k(cand, PEER_TOPK)
        idx = jnp.take_along_axis(cand_idx, pos, axis=-1)
        g = jax.nn.softmax(top_s, axis=-1)
        u = expert_u[idx]
        act = jax.nn.gelu(jnp.einsum('td,thkd->thk', xb, u).astype(jnp.float32))
        return jnp.einsum('thk,thkd->td', (g * act).astype(xb.dtype), expert_v[idx])

    return lax.map(one, blocks).reshape(-1, D_MODEL)[:n]


def trunk_layer(h, c, pos, token_mix, w_ada, b_ada, norm1_g, w_in, ret_norm_g, w_out,
                norm2_g, peer_w_query, peer_sub_keys, peer_u, peer_v):
    sh1, sc1, gt1, sh2, sc2, gt2 = adaln(c, w_ada, b_ada)
    a = modulate(h, norm1_g, sh1, sc1)
    rq, rk, rv, rg, sq, sk, sv = split_heads(a @ w_in)
    rq = rope(rq, pos) * (RET_DK ** -0.5)
    rk = rope(rk, pos)
    ro, so, new_state = token_mix(rq, rk, rv, sq, sk, sv)
    h = h + gt1 * merge_groups(ro, rg, so, ret_norm_g, w_out)
    b = modulate(h, norm2_g, sh2, sc2)
    B, L, D = b.shape
    h = h + gt2 * peer_ffn(b.reshape(B * L, D), peer_w_query, peer_sub_keys, peer_u, peer_v).reshape(B, L, D)
    return h, new_state


def setup_inputs(seed: int = 0) -> dict:
    key = jax.random.key(seed)
    ks = jax.random.split(key, 20)

    def nrm(k, shape, s):
        return jax.random.normal(k, shape, jnp.float32) * s

    return {
        'x_prompt': nrm(ks[0], (BATCH, SEQ, D_MODEL), 1.0),
        'x_sample': nrm(ks[1], (DEC_BATCH, DEC_SEQ, D_MODEL), 1.0),
        'c_prompt': nrm(ks[2], (BATCH, D_MODEL), 1.0),
        'c_sample': nrm(ks[3], (DEC_BATCH, D_MODEL), 1.0),
        'state_ret': nrm(ks[4], (DEPTH, DEC_BATCH, RET_HEADS, RET_DK, RET_DV), 1.0),
        'cache_sb_k': nrm(ks[5], (DEPTH, DEC_BATCH, PAST_LEN, SB_HEADS, SB_DH), 1.0),
        'cache_sb_v': nrm(ks[6], (DEPTH, DEC_BATCH, PAST_LEN, SB_HEADS, SB_DH), 1.0),
        'w_ada': nrm(ks[7], (DEPTH, D_MODEL, 6 * D_MODEL), 0.5 * D_MODEL ** -0.5),
        'b_ada': nrm(ks[8], (DEPTH, 6 * D_MODEL), 0.01),
        'norm1_g': 1.0 + nrm(ks[9], (DEPTH, D_MODEL), 0.02),
        'w_in': nrm(ks[10], (DEPTH, D_MODEL, IN_COLS), D_MODEL ** -0.5),
        'ret_norm_g': 1.0 + nrm(ks[11], (DEPTH, RET_WV), 0.02),
        'w_out': nrm(ks[12], (DEPTH, D_MIX, D_MODEL), D_MIX ** -0.5),
        'norm2_g': 1.0 + nrm(ks[13], (DEPTH, D_MODEL), 0.02),
        'peer_w_query': nrm(ks[14], (DEPTH, D_MODEL, PEER_HEADS * PEER_DKEY), D_MODEL ** -0.5),
        'peer_sub_keys': nrm(ks[15], (DEPTH, 2, PEER_HEADS, PEER_NKEYS, PEER_DKEY // 2), (PEER_DKEY // 2) ** -0.5),
        'peer_u': nrm(ks[16], (DEPTH, PEER_EXPERTS, D_MODEL), D_MODEL ** -0.5),
        'peer_v': nrm(ks[17], (DEPTH, PEER_EXPERTS, D_MODEL), 0.5),
        'final_norm_g': 1.0 + nrm(ks[18], (D_MODEL,), 0.02),
    }


def reference(x_prompt, x_sample, c_prompt, c_sample, state_ret, cache_sb_k, cache_sb_v,
              w_ada, b_ada, norm1_g, w_in, ret_norm_g, w_out, norm2_g,
              peer_w_query, peer_sub_keys, peer_u, peer_v, final_norm_g):
    hp, hs = x_prompt, x_sample
    pos_p = jnp.arange(x_prompt.shape[1])
    pos_s = cache_sb_k.shape[2] + jnp.arange(x_sample.shape[1])
    ret_p, k_p, v_p, ret_s, k_s, v_s = [], [], [], [], [], []
    for l in range(DEPTH):
        lp = (w_ada[l], b_ada[l], norm1_g[l], w_in[l], ret_norm_g[l], w_out[l], norm2_g[l],
              peer_w_query[l], peer_sub_keys[l], peer_u[l], peer_v[l])
        hp, (sp_, kp_, vp_) = trunk_layer(hp, c_prompt, pos_p, mix_first_chunk, *lp)
        mix_s = functools.partial(mix_later_chunk, state_ret[l], cache_sb_k[l], cache_sb_v[l])
        hs, (ss_, ks_, vs_) = trunk_layer(hs, c_sample, pos_s, mix_s, *lp)
        ret_p.append(sp_)
        k_p.append(kp_)
        v_p.append(vp_)
        ret_s.append(ss_)
        k_s.append(ks_)
        v_s.append(vs_)
    y_prompt = rmsnorm(hp, final_norm_g)
    y_sample = rmsnorm(hs, final_norm_g)
    return (y_prompt, y_sample,
            jnp.stack(ret_p), jnp.stack(k_p), jnp.stack(v_p),
            jnp.stack(ret_s), jnp.stack(k_s), jnp.stack(v_s))
```

```python
import functools

import jax
import jax.numpy as jnp
from jax import lax
from jax.experimental import pallas as pl
from jax.experimental.pallas import tpu as pltpu

F32 = jnp.float32
BF16 = jnp.bfloat16

NORM_EPS = 1e-6
ROPE_BASE = 10000.0
HEAD_DIM = 128
RET_HEADS = 4
SB_HEADS = 4
MIX_W = RET_HEADS * HEAD_DIM
PEER_HEADS = 8
PEER_TOPK = 16
SB_QBLOCK = 128
RET_CHUNK = 128
VMEM_LIMIT = 56 * 1024 * 1024
NEG_INF = float("-inf")


def _cparams(*sem):
    return pltpu.CompilerParams(dimension_semantics=sem, vmem_limit_bytes=VMEM_LIMIT)


def _dot(a, b):
    return jnp.dot(a, b, preferred_element_type=F32)


def _dot_nt(a, b):
    return lax.dot_general(a, b, (((1,), (1,)), ((), ())), preferred_element_type=F32)


def _split_bf16(x):
    hi = x.astype(BF16)
    lo = (x - hi.astype(F32)).astype(BF16)
    return hi, lo


def _rms(x):
    return x * lax.rsqrt(jnp.mean(x * x, axis=-1, keepdims=True) + NORM_EPS)


def _silu(x):
    return x / (1.0 + jnp.exp(-x))


def _adaln_kernel(c_ref, w_ref, b_ref, o_ref):
    s = _silu(c_ref[...])
    s_hi, s_lo = _split_bf16(s)
    w_hi, w_lo = _split_bf16(w_ref[...])
    o_ref[...] = _dot(s_hi, w_hi) + _dot(s_hi, w_lo) + _dot(s_lo, w_hi) + b_ref[...]


def _adaln(c, w_ada, b_ada):
    nb, d = c.shape
    n = w_ada.shape[1]
    tn = 1024
    return pl.pallas_call(
        _adaln_kernel,
        out_shape=jax.ShapeDtypeStruct((nb, n), F32),
        grid=(n // tn,),
        in_specs=[pl.BlockSpec((nb, d), lambda j: (0, 0)),
                  pl.BlockSpec((d, tn), lambda j: (0, j)),
                  pl.BlockSpec((1, tn), lambda j: (0, j))],
        out_specs=pl.BlockSpec((nb, tn), lambda j: (0, j)),
        compiler_params=_cparams("arbitrary"),
        name="adaln",
    )(c, w_ada, b_ada.reshape(1, n))


def _mod_blockspec(tm, rows_per_batch, d, ngrid):
    groups = max(1, tm // rows_per_batch)
    if groups == 1:
        per = rows_per_batch // tm
        if ngrid == 1:
            return groups, pl.BlockSpec((1, 6, d), lambda i: (i // per, 0, 0))
        return groups, pl.BlockSpec((1, 6, d), lambda i, k: (i // per, 0, 0))
    if ngrid == 1:
        return groups, pl.BlockSpec((groups, 6, d), lambda i: (i, 0, 0))
    return groups, pl.BlockSpec((groups, 6, d), lambda i, k: (i, 0, 0))


def _inproj_kernel(x_ref, mod_ref, g1_ref, w_ref, cos_ref, sin_ref,
                   rq_ref, rk_ref, rv_ref, rg_ref, sq_ref, sk_ref, sv_ref, a_scr, *, groups, rows):
    for g in range(groups):
        r = slice(g * rows, (g + 1) * rows)
        xn = _rms(x_ref[r, :]) * g1_ref[...]
        a = xn * (1.0 + mod_ref[g, 1:2, :]) + mod_ref[g, 0:1, :]
        a_scr[r, :] = a.astype(BF16)
    y = _dot(a_scr[...], w_ref[...])
    cos = cos_ref[...]
    sin = sin_ref[...]
    for h in range(RET_HEADS):
        c = slice(h * HEAD_DIM, (h + 1) * HEAD_DIM)
        yq = y[:, h * HEAD_DIM:(h + 1) * HEAD_DIM]
        yk = y[:, MIX_W + h * HEAD_DIM:MIX_W + (h + 1) * HEAD_DIM]
        rq = (yq * cos + pltpu.roll(yq, HEAD_DIM // 2, 1) * sin) * (HEAD_DIM ** -0.5)
        rk = yk * cos + pltpu.roll(yk, HEAD_DIM // 2, 1) * sin
        rq_ref[:, c] = rq.astype(rq_ref.dtype)
        rk_ref[:, c] = rk.astype(rk_ref.dtype)
    rv_ref[...] = y[:, 2 * MIX_W:3 * MIX_W].astype(rv_ref.dtype)
    rg_ref[...] = y[:, 3 * MIX_W:4 * MIX_W].astype(rg_ref.dtype)
    sq_ref[...] = y[:, 4 * MIX_W:5 * MIX_W].astype(sq_ref.dtype)
    sk_ref[...] = y[:, 5 * MIX_W:6 * MIX_W]
    sv_ref[...] = y[:, 6 * MIX_W:7 * MIX_W]


def _inproj(x, mod, g1, w_in, cos_t, sin_t, rows_per_batch, tm):
    n, d = x.shape
    ncol = w_in.shape[1]
    groups, mod_spec = _mod_blockspec(tm, rows_per_batch, d, 1)
    npos = cos_t.shape[0] // tm
    tok = lambda dt: jax.ShapeDtypeStruct((n, MIX_W), dt)
    tspec = pl.BlockSpec((tm, MIX_W), lambda i: (i, 0))
    return pl.pallas_call(
        functools.partial(_inproj_kernel, groups=groups, rows=tm // groups),
        out_shape=(tok(BF16), tok(BF16), tok(BF16), tok(BF16), tok(BF16), tok(F32), tok(F32)),
        grid=(n // tm,),
        in_specs=[pl.BlockSpec((tm, d), lambda i: (i, 0)),
                  mod_spec,
                  pl.BlockSpec((1, d), lambda i: (0, 0)),
                  pl.BlockSpec((d, ncol), lambda i: (0, 0)),
                  pl.BlockSpec((tm, HEAD_DIM), lambda i: (i % npos, 0)),
                  pl.BlockSpec((tm, HEAD_DIM), lambda i: (i % npos, 0))],
        out_specs=(tspec,) * 7,
        scratch_shapes=[pltpu.VMEM((tm, d), BF16)],
        compiler_params=_cparams("arbitrary"),
        name="inproj",
    )(x, mod, g1, w_in, cos_t, sin_t)


def _retention_kernel(q_ref, k_ref, v_ref, g_ref, s0_ref, intra_ref, qd_ref, kd_ref, cd_ref, gn_ref,
                      o_ref, sout_ref, s_scr):
    c = pl.program_id(1)

    @pl.when(c == 0)
    def _():
        s_scr[...] = s0_ref[0]

    for h in range(RET_HEADS):
        cs = slice(h * HEAD_DIM, (h + 1) * HEAD_DIM)
        q = q_ref[:, cs]
        k = k_ref[:, cs]
        v = v_ref[:, cs]
        state = s_scr[h]
        scores = _dot_nt(q, k) * intra_ref[h]
        o = _dot(scores.astype(BF16), v) + _dot(q, state.astype(BF16)) * qd_ref[h]
        kdt = (k.astype(F32) * kd_ref[h]).T.astype(BF16)
        s_scr[h] = cd_ref[h] * state + _dot(kdt, v)
        cen = o - jnp.mean(o, axis=-1, keepdims=True)
        on = cen * lax.rsqrt(jnp.mean(cen * cen, axis=-1, keepdims=True) + NORM_EPS)
        gate = _silu(g_ref[:, cs].astype(F32))
        o_ref[:, cs] = (on * gn_ref[:, cs] * gate).astype(o_ref.dtype)

    @pl.when(c == pl.num_programs(1) - 1)
    def _():
        sout_ref[0] = s_scr[...]


def _retention_tables(chunk):
    log_g = jnp.log1p(-jnp.exp2(-5.0 - jnp.arange(RET_HEADS, dtype=F32)))
    idx = jnp.arange(chunk, dtype=F32)
    diff = idx[:, None] - idx[None, :]
    causal = diff >= 0
    intra = jnp.where(causal[None], jnp.exp(jnp.where(causal, diff, 0.0)[None] * log_g[:, None, None]), 0.0)
    qd = jnp.exp((idx[None, :] + 1.0) * log_g[:, None])
    kd = jnp.exp((chunk - 1.0 - idx)[None, :] * log_g[:, None])
    cd = jnp.exp(chunk * log_g)
    bc = lambda t: jnp.broadcast_to(t[:, :, None], (RET_HEADS, chunk, HEAD_DIM))
    return intra, bc(qd), bc(kd), jnp.broadcast_to(cd[:, None, None], (RET_HEADS, 1, HEAD_DIM))


def _retention(rq, rk, rv, rg, s0, ret_norm_g, nbatch, seq, chunk):
    n = rq.shape[0]
    nc = seq // chunk
    intra, qd, kd, cd = _retention_tables(chunk)
    tspec = pl.BlockSpec((chunk, MIX_W), lambda b, c: (b * nc + c, 0))
    full = lambda shape: pl.BlockSpec(shape, lambda b, c: (0,) * len(shape))
    sspec = pl.BlockSpec((1, RET_HEADS, HEAD_DIM, HEAD_DIM), lambda b, c: (b, 0, 0, 0))
    return pl.pallas_call(
        _retention_kernel,
        out_shape=(jax.ShapeDtypeStruct((n, MIX_W), BF16),
                   jax.ShapeDtypeStruct((nbatch, RET_HEADS, HEAD_DIM, HEAD_DIM), F32)),
        grid=(nbatch, nc),
        in_specs=[tspec, tspec, tspec, tspec, sspec,
                  full((RET_HEADS, chunk, chunk)), full((RET_HEADS, chunk, HEAD_DIM)),
                  full((RET_HEADS, chunk, HEAD_DIM)), full((RET_HEADS, 1, HEAD_DIM)), full((1, MIX_W))],
        out_specs=(tspec, sspec),
        scratch_shapes=[pltpu.VMEM((RET_HEADS, HEAD_DIM, HEAD_DIM), F32)],
        compiler_params=_cparams("arbitrary", "arbitrary"),
        name="retention",
    )(rq, rk, rv, rg, s0, intra, qd, kd, cd, ret_norm_g)


def _log_sigmoid(z):
    return jnp.minimum(z, 0.0) - jnp.log1p(jnp.exp(-jnp.abs(z)))


def _sb_kernel(q_ref, kd_ref, vd_ref, kp_ref, vp_ref, trid_ref, trip_ref, o_ref, kpad, vpad,
               *, qb, kb, n_past):
    q = q_ref[...]
    scale = HEAD_DIM ** -0.5

    def block(kblk, vblk, tri, carry, mask):
        z = _dot_nt(q, kblk) * scale
        log_beta = _log_sigmoid(z)
        log_keep = log_beta - z
        if mask is not None:
            log_keep = jnp.where(mask, log_keep, 0.0)
        hi, lo = _split_bf16(log_keep)
        after = _dot(hi, tri) + _dot(lo, tri) + carry
        a = jnp.exp(log_beta + after)
        if mask is not None:
            a = jnp.where(mask, a, 0.0)
        return _dot(a.astype(BF16), vblk), carry + jnp.sum(log_keep, axis=1, keepdims=True)

    if qb == SB_QBLOCK:
        kdiag = kd_ref[...].astype(BF16)
        vdiag = vd_ref[...].astype(BF16)
    else:
        kpad[...] = jnp.zeros_like(kpad)
        vpad[...] = jnp.zeros_like(vpad)
        kpad[0:qb, :] = kd_ref[...].astype(BF16)
        vpad[0:qb, :] = vd_ref[...].astype(BF16)
        kdiag = kpad[...]
        vdiag = vpad[...]
    tq = lax.broadcasted_iota(jnp.int32, (qb, SB_QBLOCK), 0)
    ts = lax.broadcasted_iota(jnp.int32, (qb, SB_QBLOCK), 1)
    out, carry = block(kdiag, vdiag, trid_ref[...], jnp.zeros((qb, 1), F32), ts < tq)

    n = pl.program_id(2) if n_past is None else n_past

    def body(it, state):
        out, carry = state
        start = pl.multiple_of((n - 1 - it) * kb, kb)
        kblk = kp_ref[0, pl.ds(start, kb), :].astype(BF16)
        vblk = vp_ref[0, pl.ds(start, kb), :].astype(BF16)
        o, carry = block(kblk, vblk, trip_ref[...], carry, None)
        return out + o, carry

    out, _ = lax.fori_loop(0, n, body, (out, carry))
    o_ref[...] = out.astype(o_ref.dtype)


def _tri(n):
    later = lax.broadcasted_iota(jnp.int32, (n, n), 0) > lax.broadcasted_iota(jnp.int32, (n, n), 1)
    return later.astype(BF16)


def _stick_breaking(sq, sk, sv, k_past, v_past, nbatch, qb, kb, n_past):
    n = sq.shape[0]
    nq = n // (nbatch * qb)
    past_len = k_past.shape[1]
    tspec = pl.BlockSpec((qb, HEAD_DIM), lambda b, h, i: (b * nq + i, h))
    pspec = pl.BlockSpec((1, past_len, HEAD_DIM), lambda b, h, i: (b, 0, h))
    return pl.pallas_call(
        functools.partial(_sb_kernel, qb=qb, kb=kb, n_past=n_past),
        out_shape=jax.ShapeDtypeStruct((n, MIX_W), BF16),
        grid=(nbatch, SB_HEADS, nq),
        in_specs=[tspec, tspec, tspec, pspec, pspec,
                  pl.BlockSpec((SB_QBLOCK, SB_QBLOCK), lambda b, h, i: (0, 0)),
                  pl.BlockSpec((kb, kb), lambda b, h, i: (0, 0))],
        out_specs=tspec,
        scratch_shapes=[pltpu.VMEM((SB_QBLOCK, HEAD_DIM), BF16), pltpu.VMEM((SB_QBLOCK, HEAD_DIM), BF16)],
        compiler_params=_cparams("arbitrary", "arbitrary", "arbitrary"),
        name="stick_breaking",
    )(sq, sk, sv, k_past, v_past, _tri(SB_QBLOCK), _tri(kb))


def _post_kernel(x_ref, ret_ref, so_ref, mod_ref, wo_ref, g2_ref, wqt_ref, keys_ref,
                 h1_ref, bt_ref, st_ref, b_scr, *, groups, rows):
    attn = _dot(ret_ref[...], wo_ref[0:MIX_W, :]) + _dot(so_ref[...], wo_ref[MIX_W:2 * MIX_W, :])
    for g in range(groups):
        r = slice(g * rows, (g + 1) * rows)
        h1 = x_ref[r, :] + mod_ref[g, 2:3, :] * attn[r, :]
        h1_ref[r, :] = h1
        b_scr[r, :] = _rms(h1) * g2_ref[...] * (1.0 + mod_ref[g, 4:5, :]) + mod_ref[g, 3:4, :]
    bt = b_scr[...].T.astype(BF16)
    bt_ref[...] = bt
    qt = _dot(wqt_ref[...], bt).astype(BF16)
    for hp in range(2 * PEER_HEADS):
        rs = slice(hp * HEAD_DIM, (hp + 1) * HEAD_DIM)
        st_ref[rs, :] = _dot(keys_ref[hp % 2, hp // 2], qt[rs, :])


def _post(x, ret, so, mod, w_out, g2, wq_t, keys, rows_per_batch, tm):
    n, d = x.shape
    nq = wq_t.shape[0]
    groups, mod_spec = _mod_blockspec(tm, rows_per_batch, d, 1)
    return pl.pallas_call(
        functools.partial(_post_kernel, groups=groups, rows=tm // groups),
        out_shape=(jax.ShapeDtypeStruct((n, d), F32),
                   jax.ShapeDtypeStruct((d, n), BF16),
                   jax.ShapeDtypeStruct((nq, n), F32)),
        grid=(n // tm,),
        in_specs=[pl.BlockSpec((tm, d), lambda i: (i, 0)),
                  pl.BlockSpec((tm, MIX_W), lambda i: (i, 0)),
                  pl.BlockSpec((tm, MIX_W), lambda i: (i, 0)),
                  mod_spec,
                  pl.BlockSpec((2 * MIX_W, d), lambda i: (0, 0)),
                  pl.BlockSpec((1, d), lambda i: (0, 0)),
                  pl.BlockSpec((nq, d), lambda i: (0, 0)),
                  pl.BlockSpec(keys.shape, lambda i: (0, 0, 0, 0))],
        out_specs=(pl.BlockSpec((tm, d), lambda i: (i, 0)),
                   pl.BlockSpec((d, tm), lambda i: (0, i)),
                   pl.BlockSpec((nq, tm), lambda i: (0, i))),
        scratch_shapes=[pltpu.VMEM((tm, d), F32)],
        compiler_params=_cparams("arbitrary"),
        name="post_mix",
    )(x, ret, so, mod, w_out, g2, wq_t, keys)


def _stack_rows(rows):
    t = rows[0].shape[1]
    ridx = lax.broadcasted_iota(jnp.int32, (8, t), 0)
    out = jnp.broadcast_to(rows[0], (8, t))
    for r in range(1, len(rows)):
        out = jnp.where(ridx == r, rows[r], out)
    return out


def _top_values(x, count):
    vals = []
    for _ in range(count):
        m = jnp.max(x, axis=0, keepdims=True)
        vals.append(m)
        x = jnp.where(x == m, NEG_INF, x)
    return vals


def _topk_kernel(s_ref, e_ref, tau_ref):
    tt = s_ref.shape[1]
    ridx = lax.broadcasted_iota(jnp.int32, (8, tt), 0)
    for h in range(PEER_HEADS):
        base = 2 * h * HEAD_DIM
        s1 = s_ref[base:base + HEAD_DIM, :]
        s2 = s_ref[base + HEAD_DIM:base + 2 * HEAD_DIM, :]
        a = _top_values(s1, PEER_TOPK)
        b = _top_values(s2, PEER_TOPK)
        a_lo, a_hi = _stack_rows(a[0:8]), _stack_rows(a[8:16])
        b_lo, b_hi = _stack_rows(b[0:8]), _stack_rows(b[8:16])
        b_mid = jnp.where(ridx >= 5, b_lo, NEG_INF)
        cand = jnp.concatenate(
            [a_lo + b[l] for l in range(5)] + [a_hi + b[0], b_hi + a[0], b_mid + a[0], b_mid + a[1]], axis=0)
        top = _top_values(cand, PEER_TOPK)
        z = jnp.zeros_like(top[0])
        for v in top:
            z = z + jnp.exp(v - top[0])
        e_ref[base:base + HEAD_DIM, :] = jnp.exp(s1 - a[0]) / z
        e_ref[base + HEAD_DIM:base + 2 * HEAD_DIM, :] = jnp.exp(s2 - b[0])
        tau_ref[h:h + 1, :] = top[PEER_TOPK - 1]


def _topk_stats(st, tt):
    nq, n = st.shape
    return pl.pallas_call(
        _topk_kernel,
        out_shape=(jax.ShapeDtypeStruct((nq, n), F32), jax.ShapeDtypeStruct((PEER_HEADS, n), F32)),
        grid=(n // tt,),
        in_specs=[pl.BlockSpec((nq, tt), lambda i: (0, i))],
        out_specs=(pl.BlockSpec((nq, tt), lambda i: (0, i)), pl.BlockSpec((PEER_HEADS, tt), lambda i: (0, i))),
        compiler_params=_cparams("arbitrary"),
        name="peer_topk",
    )(st)


def _gelu(x):
    return 0.5 * x * (1.0 + jnp.tanh(0.7978845608028654 * (x + 0.044715 * (x * x * x))))


def _peer_kernel(bt_ref, u_ref, vt_ref, s_ref, e_ref, tau_ref, h1_ref, mod_ref, gf_ref, y_ref,
                 acc_ref, act_scr, wa_scr, *, groups, rows, ec):
    k = pl.program_id(1)
    ngrp = ec // HEAD_DIM

    @pl.when(k == 0)
    def _():
        acc_ref[...] = jnp.zeros_like(acc_ref)

    act_scr[...] = _gelu(_dot(u_ref[...], bt_ref[...]))

    def group(il, carry):
        ig = k * ngrp + il
        r0 = pl.multiple_of(il * HEAD_DIM, HEAD_DIM)
        w = jnp.zeros((HEAD_DIM, act_scr.shape[1]), F32)
        for h in range(PEER_HEADS):
            base = 2 * h * HEAD_DIM
            s1 = s_ref[pl.ds(base + ig, 1), :]
            e1 = e_ref[pl.ds(base + ig, 1), :]
            cand = s_ref[base + HEAD_DIM:base + 2 * HEAD_DIM, :] + s1
            gate = e_ref[base + HEAD_DIM:base + 2 * HEAD_DIM, :] * e1
            w = w + jnp.where(cand >= tau_ref[h:h + 1, :], gate, 0.0)
        wa_scr[pl.ds(r0, HEAD_DIM), :] = (w * act_scr[pl.ds(r0, HEAD_DIM), :]).astype(BF16)
        return carry

    lax.fori_loop(0, ngrp, group, 0)
    acc_ref[...] += _dot(vt_ref[...], wa_scr[...])

    @pl.when(k == pl.num_programs(1) - 1)
    def _():
        out = acc_ref[...].T
        for g in range(groups):
            r = slice(g * rows, (g + 1) * rows)
            h2 = h1_ref[r, :] + mod_ref[g, 5:6, :] * out[r, :]
            y_ref[r, :] = _rms(h2) * gf_ref[...]


def _peer(bt, u, vt, st, et, tau, h1, mod, gf, rows_per_batch, t, ec):
    d, n = bt.shape
    ne = u.shape[0]
    nq = st.shape[0]
    groups, mod_spec = _mod_blockspec(t, rows_per_batch, d, 2)
    return pl.pallas_call(
        functools.partial(_peer_kernel, groups=groups, rows=t // groups, ec=ec),
        out_shape=jax.ShapeDtypeStruct((n, d), F32),
        grid=(n // t, ne // ec),
        in_specs=[pl.BlockSpec((d, t), lambda i, k: (0, i)),
                  pl.BlockSpec((ec, d), lambda i, k: (k, 0)),
                  pl.BlockSpec((d, ec), lambda i, k: (0, k)),
                  pl.BlockSpec((nq, t), lambda i, k: (0, i)),
                  pl.BlockSpec((nq, t), lambda i, k: (0, i)),
                  pl.BlockSpec((PEER_HEADS, t), lambda i, k: (0, i)),
                  pl.BlockSpec((t, d), lambda i, k: (i, 0)),
                  mod_spec,
                  pl.BlockSpec((1, d), lambda i, k: (0, 0))],
        out_specs=pl.BlockSpec((t, d), lambda i, k: (i, 0)),
        scratch_shapes=[pltpu.VMEM((d, t), F32), pltpu.VMEM((ec, t), F32), pltpu.VMEM((ec, t), BF16)],
        compiler_params=_cparams("arbitrary", "arbitrary"),
        name="peer_dense",
    )(bt, u, vt, st, et, tau, h1, mod, gf)


def _rope_tables(pos):
    half = HEAD_DIM // 2
    inv_freq = ROPE_BASE ** (-jnp.arange(half, dtype=F32) / half)
    ang = pos.astype(F32)[:, None] * inv_freq[None, :]
    cos, sin = jnp.cos(ang), jnp.sin(ang)
    return jnp.concatenate([cos, cos], axis=1), jnp.concatenate([-sin, sin], axis=1)


def _tile(n, pref):
    t = min(n, pref)
    assert n % t == 0
    return t


def _run_group(x, mod, pos, s0, k_past, v_past, params):
    (g1, w_in, ret_g, w_out, g2, wq_t, keys, u, vt, gf) = params
    nbatch, seq, d = x.shape
    n = nbatch * seq
    x2 = x.reshape(n, d)

    tm = _tile(n, 256)
    reps = max(1, tm // seq)
    cos_t, sin_t = _rope_tables(pos)
    cos_t, sin_t = jnp.tile(cos_t, (reps, 1)), jnp.tile(sin_t, (reps, 1))
    rq, rk, rv, rg, sq, sk, sv = _inproj(x2, mod, g1, w_in, cos_t, sin_t, seq, tm)

    chunk = min(seq, RET_CHUNK)
    ret, s_new = _retention(rq, rk, rv, rg, s0, ret_g, nbatch, seq, chunk)

    if k_past is None:
        so = _stick_breaking(sq, sk, sv, sk.reshape(nbatch, seq, MIX_W), sv.reshape(nbatch, seq, MIX_W),
                             nbatch, SB_QBLOCK, SB_QBLOCK, None)
    else:
        past_len = k_past.shape[1]
        kb = _tile(past_len, 512)
        so = _stick_breaking(sq, sk, sv, k_past.reshape(nbatch, past_len, MIX_W),
                             v_past.reshape(nbatch, past_len, MIX_W), nbatch, seq, kb, past_len // kb)

    tp = _tile(n, 512)
    h1, bt, st = _post(x2, ret, so, mod, w_out, g2, wq_t, keys, seq, tp)
    et, tau = _topk_stats(st, _tile(n, 256))
    y = _peer(bt, u, vt, st, et, tau, h1, mod, gf, seq, tp, 1024)
    return (y.reshape(nbatch, seq, d),
            s_new.reshape(1, nbatch, RET_HEADS, HEAD_DIM, HEAD_DIM),
            sk.reshape(1, nbatch, seq, SB_HEADS, HEAD_DIM),
            sv.reshape(1, nbatch, seq, SB_HEADS, HEAD_DIM))


def kernel(x_prompt, x_sample, c_prompt, c_sample, state_ret, cache_sb_k, cache_sb_v, w_ada, b_ada, norm1_g,
           w_in, ret_norm_g, w_out, norm2_g, peer_w_query, peer_sub_keys, peer_u, peer_v, final_norm_g):
    assert w_ada.shape[0] == 1, "single-layer step"
    bp, seq_p, d = x_prompt.shape
    bs, seq_s, _ = x_sample.shape
    past_len = cache_sb_k.shape[2]

    mod = _adaln(jnp.concatenate([c_prompt, c_sample], axis=0), w_ada[0], b_ada[0]).reshape(bp + bs, 6, d)
    params = (norm1_g[0].reshape(1, d), w_in[0].astype(BF16), ret_norm_g[0].reshape(1, MIX_W),
              w_out[0].astype(BF16), norm2_g[0].reshape(1, d), peer_w_query[0].T.astype(BF16),
              peer_sub_keys[0].astype(BF16), peer_u[0].astype(BF16), peer_v[0].T.astype(BF16),
              final_norm_g.reshape(1, d))

    zeros_state = jnp.zeros((bp, RET_HEADS, HEAD_DIM, HEAD_DIM), F32)
    y_p, s_p, k_p, v_p = _run_group(x_prompt, mod[:bp], jnp.arange(seq_p), zeros_state, None, None, params)
    y_s, s_s, k_s, v_s = _run_group(x_sample, mod[bp:], past_len + jnp.arange(seq_s), state_ret[0],
                                    cache_sb_k[0], cache_sb_v[0], params)
    return (y_p, y_s, s_p, k_p, v_p, s_s, k_s, v_s)
```

```python
import functools

import jax
import jax.numpy as jnp
from jax import lax
from jax.experimental import pallas as pl
from jax.experimental.pallas import tpu as pltpu

F32 = jnp.float32
BF16 = jnp.bfloat16

NORM_EPS = 1e-6
ROPE_BASE = 10000.0
HEAD_DIM = 128
RET_HEADS = 4
SB_HEADS = 4
MIX_W = RET_HEADS * HEAD_DIM
PEER_HEADS = 8
PEER_TOPK = 16
SB_QBLOCK = 128
RET_CHUNK = 128
VMEM_LIMIT = 56 * 1024 * 1024
NEG_INF = float("-inf")


def _cparams(*sem):
    return pltpu.CompilerParams(dimension_semantics=sem, vmem_limit_bytes=VMEM_LIMIT)


def _dot(a, b):
    return jnp.dot(a, b, preferred_element_type=F32)


def _dot_nt(a, b):
    return lax.dot_general(a, b, (((1,), (1,)), ((), ())), preferred_element_type=F32)


def _split_bf16(x):
    hi = x.astype(BF16)
    lo = (x - hi.astype(F32)).astype(BF16)
    return hi, lo


def _rms(x):
    return x * lax.rsqrt(jnp.mean(x * x, axis=-1, keepdims=True) + NORM_EPS)


def _silu(x):
    return x / (1.0 + jnp.exp(-x))


def _adaln_kernel(c_ref, w_ref, b_ref, o_ref):
    s = _silu(c_ref[...])
    s_hi, s_lo = _split_bf16(s)
    w_hi, w_lo = _split_bf16(w_ref[...])
    o_ref[...] = _dot(s_hi, w_hi) + _dot(s_hi, w_lo) + _dot(s_lo, w_hi) + b_ref[...]


def _adaln(c, w_ada, b_ada):
    nb, d = c.shape
    n = w_ada.shape[1]
    tn = 1024
    return pl.pallas_call(
        _adaln_kernel,
        out_shape=jax.ShapeDtypeStruct((nb, n), F32),
        grid=(n // tn,),
        in_specs=[pl.BlockSpec((nb, d), lambda j: (0, 0)),
                  pl.BlockSpec((d, tn), lambda j: (0, j)),
                  pl.BlockSpec((1, tn), lambda j: (0, j))],
        out_specs=pl.BlockSpec((nb, tn), lambda j: (0, j)),
        compiler_params=_cparams("arbitrary"),
        name="adaln",
    )(c, w_ada, b_ada.reshape(1, n))


def _mod_blockspec(tm, rows_per_batch, d, ngrid):
    groups = max(1, tm // rows_per_batch)
    if groups == 1:
        per = rows_per_batch // tm
        if ngrid == 1:
            return groups, pl.BlockSpec((1, 6, d), lambda i: (i // per, 0, 0))
        return groups, pl.BlockSpec((1, 6, d), lambda i, k: (i // per, 0, 0))
    if ngrid == 1:
        return groups, pl.BlockSpec((groups, 6, d), lambda i: (i, 0, 0))
    return groups, pl.BlockSpec((groups, 6, d), lambda i, k: (i, 0, 0))


def _inproj_kernel(x_ref, mod_ref, g1_ref, w_ref, cos_ref, sin_ref,
                   rq_ref, rk_ref, rv_ref, rg_ref, sq_ref, sk_ref, sv_ref, a_scr, *, groups, rows):
    for g in range(groups):
        r = slice(g * rows, (g + 1) * rows)
        xn = _rms(x_ref[r, :]) * g1_ref[...]
        a = xn * (1.0 + mod_ref[g, 1:2, :]) + mod_ref[g, 0:1, :]
        a_scr[r, :] = a.astype(BF16)
    y = _dot(a_scr[...], w_ref[...])
    cos = cos_ref[...]
    sin = sin_ref[...]
    for h in range(RET_HEADS):
        c = slice(h * HEAD_DIM, (h + 1) * HEAD_DIM)
        yq = y[:, h * HEAD_DIM:(h + 1) * HEAD_DIM]
        yk = y[:, MIX_W + h * HEAD_DIM:MIX_W + (h + 1) * HEAD_DIM]
        rq = (yq * cos + pltpu.roll(yq, HEAD_DIM // 2, 1) * sin) * (HEAD_DIM ** -0.5)
        rk = yk * cos + pltpu.roll(yk, HEAD_DIM // 2, 1) * sin
        rq_ref[:, c] = rq.astype(rq_ref.dtype)
        rk_ref[:, c] = rk.astype(rk_ref.dtype)
    rv_ref[...] = y[:, 2 * MIX_W:3 * MIX_W].astype(rv_ref.dtype)
    rg_ref[...] = y[:, 3 * MIX_W:4 * MIX_W].astype(rg_ref.dtype)
    sq_ref[...] = y[:, 4 * MIX_W:5 * MIX_W].astype(sq_ref.dtype)
    sk_ref[...] = y[:, 5 * MIX_W:6 * MIX_W]
    sv_ref[...] = y[:, 6 * MIX_W:7 * MIX_W]


def _inproj(x, mod, g1, w_in, cos_t, sin_t, rows_per_batch, tm):
    n, d = x.shape
    ncol = w_in.shape[1]
    groups, mod_spec = _mod_blockspec(tm, rows_per_batch, d, 1)
    npos = cos_t.shape[0] // tm
    tok = lambda dt: jax.ShapeDtypeStruct((n, MIX_W), dt)
    tspec = pl.BlockSpec((tm, MIX_W), lambda i: (i, 0))
    return pl.pallas_call(
        functools.partial(_inproj_kernel, groups=groups, rows=tm // groups),
        out_shape=(tok(BF16), tok(BF16), tok(BF16), tok(BF16), tok(BF16), tok(F32), tok(F32)),
        grid=(n // tm,),
        in_specs=[pl.BlockSpec((tm, d), lambda i: (i, 0)),
                  mod_spec,
                  pl.BlockSpec((1, d), lambda i: (0, 0)),
                  pl.BlockSpec((d, ncol), lambda i: (0, 0)),
                  pl.BlockSpec((tm, HEAD_DIM), lambda i: (i % npos, 0)),
                  pl.BlockSpec((tm, HEAD_DIM), lambda i: (i % npos, 0))],
        out_specs=(tspec,) * 7,
        scratch_shapes=[pltpu.VMEM((tm, d), BF16)],
        compiler_params=_cparams("arbitrary"),
        name="inproj",
    )(x, mod, g1, w_in, cos_t, sin_t)


def _retention_kernel(q_ref, k_ref, v_ref, g_ref, s0_ref, intra_ref, qd_ref, kd_ref, cd_ref, gn_ref,
                      o_ref, sout_ref, s_scr):
    c = pl.program_id(1)

    @pl.when(c == 0)
    def _():
        s_scr[...] = s0_ref[0]

    for h in range(RET_HEADS):
        cs = slice(h * HEAD_DIM, (h + 1) * HEAD_DIM)
        q = q_ref[:, cs]
        k = k_ref[:, cs]
        v = v_ref[:, cs]
        state = s_scr[h]
        scores = _dot_nt(q, k) * intra_ref[h]
        o = _dot(scores.astype(BF16), v) + _dot(q, state.astype(BF16)) * qd_ref[h]
        kdt = (k.astype(F32) * kd_ref[h]).T.astype(BF16)
        s_scr[h] = cd_ref[h] * state + _dot(kdt, v)
        cen = o - jnp.mean(o, axis=-1, keepdims=True)
        on = cen * lax.rsqrt(jnp.mean(cen * cen, axis=-1, keepdims=True) + NORM_EPS)
        gate = _silu(g_ref[:, cs].astype(F32))
        o_ref[:, cs] = (on * gn_ref[:, cs] * gate).astype(o_ref.dtype)

    @pl.when(c == pl.num_programs(1) - 1)
    def _():
        sout_ref[0] = s_scr[...]


def _retention_tables(chunk):
    log_g = jnp.log1p(-jnp.exp2(-5.0 - jnp.arange(RET_HEADS, dtype=F32)))
    idx = jnp.arange(chunk, dtype=F32)
    diff = idx[:, None] - idx[None, :]
    causal = diff >= 0
    intra = jnp.where(causal[None], jnp.exp(jnp.where(causal, diff, 0.0)[None] * log_g[:, None, None]), 0.0)
    qd = jnp.exp((idx[None, :] + 1.0) * log_g[:, None])
    kd = jnp.exp((chunk - 1.0 - idx)[None, :] * log_g[:, None])
    cd = jnp.exp(chunk * log_g)
    bc = lambda t: jnp.broadcast_to(t[:, :, None], (RET_HEADS, chunk, HEAD_DIM))
    return intra, bc(qd), bc(kd), jnp.broadcast_to(cd[:, None, None], (RET_HEADS, 1, HEAD_DIM))


def _retention(rq, rk, rv, rg, s0, ret_norm_g, nbatch, seq, chunk):
    n = rq.shape[0]
    nc = seq // chunk
    intra, qd, kd, cd = _retention_tables(chunk)
    tspec = pl.BlockSpec((chunk, MIX_W), lambda b, c: (b * nc + c, 0))
    full = lambda shape: pl.BlockSpec(shape, lambda b, c: (0,) * len(shape))
    sspec = pl.BlockSpec((1, RET_HEADS, HEAD_DIM, HEAD_DIM), lambda b, c: (b, 0, 0, 0))
    return pl.pallas_call(
        _retention_kernel,
        out_shape=(jax.ShapeDtypeStruct((n, MIX_W), BF16),
                   jax.ShapeDtypeStruct((nbatch, RET_HEADS, HEAD_DIM, HEAD_DIM), F32)),
        grid=(nbatch, nc),
        in_specs=[tspec, tspec, tspec, tspec, sspec,
                  full((RET_HEADS, chunk, chunk)), full((RET_HEADS, chunk, HEAD_DIM)),
                  full((RET_HEADS, chunk, HEAD_DIM)), full((RET_HEADS, 1, HEAD_DIM)), full((1, MIX_W))],
        out_specs=(tspec, sspec),
        scratch_shapes=[pltpu.VMEM((RET_HEADS, HEAD_DIM, HEAD_DIM), F32)],
        compiler_params=_cparams("arbitrary", "arbitrary"),
        name="retention",
    )(rq, rk, rv, rg, s0, intra, qd, kd, cd, ret_norm_g)


def _log_sigmoid(z):
    return jnp.minimum(z, 0.0) - jnp.log(1.0 + jnp.exp(-jnp.abs(z)))


SB_TILE = 512


def _sb_scores(q, kblk):
    return _dot_nt(q, kblk) * (HEAD_DIM ** -0.5)


def _sb_cumulate(z, cumw, mask):
    nsub = z.shape[1] // 128
    log_beta = _log_sigmoid(z)
    if mask is not None:
        log_beta = jnp.where(mask, log_beta, 0.0)
        z = jnp.where(mask, z, 0.0)
    parts = _split_bf16(log_beta) + _split_bf16(z)
    lhs = jnp.concatenate(
        [jnp.concatenate([p[:, c * 128:(c + 1) * 128] for p in parts], axis=1) for c in range(nsub)], axis=0)
    return _dot(lhs, cumw)


def _sb_weights(cw, vblk, carry, mask):
    nsub = vblk.shape[0] // 128
    qb = cw.shape[0] // nsub
    a = [None] * nsub
    for c in reversed(range(nsub)):
        a[c] = jnp.exp(cw[c * qb:(c + 1) * qb, 0:128] + carry)
        carry = carry + cw[c * qb:(c + 1) * qb, 128:256]
    a = jnp.concatenate(a, axis=1)
    if mask is not None:
        a = jnp.where(mask, a, 0.0)
    return _dot(a.astype(BF16), vblk), carry


def _sb_tiles(q_ref, k_of, v_of, cumw_ref, carry_of, mask):
    heads = _sb_heads()
    z, cw, res = {}, {}, {}
    for step in range(len(heads) + 2):
        if step < len(heads):
            z[step] = _sb_scores(q_ref[:, heads[step]], k_of(heads[step]))
        if 0 <= step - 1 < len(heads):
            cw[step - 1] = _sb_cumulate(z.pop(step - 1), cumw_ref[...], mask)
        if 0 <= step - 2 < len(heads):
            hs = heads[step - 2]
            res[step - 2] = _sb_weights(cw.pop(step - 2), v_of(hs), carry_of(hs), mask)
    return [res[h] for h in range(len(heads))]


def _sb_heads():
    return [slice(h * HEAD_DIM, (h + 1) * HEAD_DIM) for h in range(SB_HEADS)]


def _sb_past_tiles(q_ref, kp_ref, vp_ref, cumw_ref, out_scr, carry_scr, ntiles):
    def body(it, _):
        start = pl.multiple_of((ntiles - 1 - it) * SB_TILE, SB_TILE)
        res = _sb_tiles(q_ref, lambda hs: kp_ref[0, pl.ds(start, SB_TILE), hs].astype(BF16),
                        lambda hs: vp_ref[0, pl.ds(start, SB_TILE), hs].astype(BF16),
                        cumw_ref, lambda hs: carry_scr[:, hs], None)
        for hs, (o, carry) in zip(_sb_heads(), res):
            out_scr[:, hs] += o
            carry_scr[:, hs] = carry
        return 0

    lax.fori_loop(0, ntiles, body, 0)


def _sb_prompt_kernel(q_ref, kp_ref, vp_ref, cumw_ref, o_ref, out_scr, carry_scr):
    i = pl.program_id(1)
    qb = q_ref.shape[0]
    last = (i * qb) // SB_TILE
    start = pl.multiple_of(last * SB_TILE, SB_TILE)
    q_pos = i * qb + lax.broadcasted_iota(jnp.int32, (qb, SB_TILE), 0)
    k_pos = start + lax.broadcasted_iota(jnp.int32, (qb, SB_TILE), 1)
    res = _sb_tiles(q_ref, lambda hs: kp_ref[0, pl.ds(start, SB_TILE), hs].astype(BF16),
                    lambda hs: vp_ref[0, pl.ds(start, SB_TILE), hs].astype(BF16),
                    cumw_ref, lambda hs: jnp.zeros((qb, 128), F32), k_pos < q_pos)
    for hs, (o, carry) in zip(_sb_heads(), res):
        out_scr[:, hs] = o
        carry_scr[:, hs] = carry
    _sb_past_tiles(q_ref, kp_ref, vp_ref, cumw_ref, out_scr, carry_scr, last)
    o_ref[...] = out_scr[...].astype(o_ref.dtype)


def _sb_sample_kernel(q_ref, kn_ref, vn_ref, kp_ref, vp_ref, cumw_ref, o_ref, kpad, vpad, out_scr, carry_scr):
    qb = q_ref.shape[0]
    kpad[...] = jnp.zeros_like(kpad)
    vpad[...] = jnp.zeros_like(vpad)
    kpad[0:qb, :] = kn_ref[...]
    vpad[0:qb, :] = vn_ref[...]
    tq = lax.broadcasted_iota(jnp.int32, (qb, 128), 0)
    ts = lax.broadcasted_iota(jnp.int32, (qb, 128), 1)
    res = _sb_tiles(q_ref, lambda hs: kpad[:, hs].astype(BF16), lambda hs: vpad[:, hs].astype(BF16),
                    cumw_ref, lambda hs: jnp.zeros((qb, 128), F32), ts < tq)
    for hs, (o, carry) in zip(_sb_heads(), res):
        out_scr[:, hs] = o
        carry_scr[:, hs] = carry
    _sb_past_tiles(q_ref, kp_ref, vp_ref, cumw_ref, out_scr, carry_scr, kp_ref.shape[1] // SB_TILE)
    o_ref[...] = out_scr[...].astype(o_ref.dtype)


def _cumw():
    n = 128
    row = lax.broadcasted_iota(jnp.int32, (n, n), 0)
    col = lax.broadcasted_iota(jnp.int32, (n, n), 1)
    ones = jnp.ones((n, n), F32)
    keep = jnp.concatenate([(row >= col).astype(F32), ones], axis=1)
    drop = -jnp.concatenate([(row > col).astype(F32), ones], axis=1)
    return jnp.concatenate([keep, keep, drop, drop], axis=0).astype(BF16)


def _stick_breaking_prompt(sq, sk, sv, nbatch, seq):
    n = sq.shape[0]
    qb = SB_QBLOCK
    nq = seq // qb
    assert seq % SB_TILE == 0
    tspec = pl.BlockSpec((qb, MIX_W), lambda b, i: (b * nq + i, 0))
    pspec = pl.BlockSpec((1, seq, MIX_W), lambda b, i: (b, 0, 0))
    return pl.pallas_call(
        _sb_prompt_kernel,
        out_shape=jax.ShapeDtypeStruct((n, MIX_W), BF16),
        grid=(nbatch, nq),
        in_specs=[tspec, pspec, pspec, pl.BlockSpec((512, 256), lambda b, i: (0, 0))],
        out_specs=tspec,
        scratch_shapes=[pltpu.VMEM((qb, MIX_W), F32), pltpu.VMEM((qb, MIX_W), F32)],
        compiler_params=_cparams("arbitrary", "arbitrary"),
        name="stick_breaking",
    )(sq, sk.reshape(nbatch, seq, MIX_W), sv.reshape(nbatch, seq, MIX_W), _cumw())


def _stick_breaking_sample(sq, sk, sv, k_past, v_past, nbatch, seq):
    n = sq.shape[0]
    past_len = k_past.shape[1]
    assert seq <= 128 and past_len % SB_TILE == 0
    tspec = pl.BlockSpec((seq, MIX_W), lambda b: (b, 0))
    pspec = pl.BlockSpec((1, past_len, MIX_W), lambda b: (b, 0, 0))
    return pl.pallas_call(
        _sb_sample_kernel,
        out_shape=jax.ShapeDtypeStruct((n, MIX_W), BF16),
        grid=(nbatch,),
        in_specs=[tspec, tspec, tspec, pspec, pspec, pl.BlockSpec((512, 256), lambda b: (0, 0))],
        out_specs=tspec,
        scratch_shapes=[pltpu.VMEM((128, MIX_W), F32), pltpu.VMEM((128, MIX_W), F32),
                        pltpu.VMEM((seq, MIX_W), F32), pltpu.VMEM((seq, MIX_W), F32)],
        compiler_params=_cparams("arbitrary"),
        name="stick_breaking_cached",
    )(sq, sk, sv, k_past, v_past, _cumw())


def _post_kernel(x_ref, ret_ref, so_ref, mod_ref, wo_ref, g2_ref, wqt_ref, keys_ref,
                 h1_ref, bt_ref, st_ref, b_scr, *, groups, rows):
    attn = _dot(ret_ref[...], wo_ref[0:MIX_W, :]) + _dot(so_ref[...], wo_ref[MIX_W:2 * MIX_W, :])
    for g in range(groups):
        r = slice(g * rows, (g + 1) * rows)
        h1 = x_ref[r, :] + mod_ref[g, 2:3, :] * attn[r, :]
        h1_ref[r, :] = h1
        b_scr[r, :] = _rms(h1) * g2_ref[...] * (1.0 + mod_ref[g, 4:5, :]) + mod_ref[g, 3:4, :]
    bt = b_scr[...].T.astype(BF16)
    bt_ref[...] = bt
    qt = _dot(wqt_ref[...], bt).astype(BF16)
    for hp in range(2 * PEER_HEADS):
        rs = slice(hp * HEAD_DIM, (hp + 1) * HEAD_DIM)
        st_ref[rs, :] = _dot(keys_ref[hp % 2, hp // 2], qt[rs, :])


def _post(x, ret, so, mod, w_out, g2, wq_t, keys, rows_per_batch, tm):
    n, d = x.shape
    nq = wq_t.shape[0]
    groups, mod_spec = _mod_blockspec(tm, rows_per_batch, d, 1)
    return pl.pallas_call(
        functools.partial(_post_kernel, groups=groups, rows=tm // groups),
        out_shape=(jax.ShapeDtypeStruct((n, d), F32),
                   jax.ShapeDtypeStruct((d, n), BF16),
                   jax.ShapeDtypeStruct((nq, n), F32)),
        grid=(n // tm,),
        in_specs=[pl.BlockSpec((tm, d), lambda i: (i, 0)),
                  pl.BlockSpec((tm, MIX_W), lambda i: (i, 0)),
                  pl.BlockSpec((tm, MIX_W), lambda i: (i, 0)),
                  mod_spec,
                  pl.BlockSpec((2 * MIX_W, d), lambda i: (0, 0)),
                  pl.BlockSpec((1, d), lambda i: (0, 0)),
                  pl.BlockSpec((nq, d), lambda i: (0, 0)),
                  pl.BlockSpec(keys.shape, lambda i: (0, 0, 0, 0))],
        out_specs=(pl.BlockSpec((tm, d), lambda i: (i, 0)),
                   pl.BlockSpec((d, tm), lambda i: (0, i)),
                   pl.BlockSpec((nq, tm), lambda i: (0, i))),
        scratch_shapes=[pltpu.VMEM((tm, d), F32)],
        compiler_params=_cparams("arbitrary"),
        name="post_mix",
    )(x, ret, so, mod, w_out, g2, wq_t, keys)


def _stack_rows(rows):
    t = rows[0].shape[1]
    ridx = lax.broadcasted_iota(jnp.int32, (8, t), 0)
    out = jnp.broadcast_to(rows[0], (8, t))
    for r in range(1, len(rows)):
        out = jnp.where(ridx == r, rows[r], out)
    return out


def _top_values(x, count, with_rank=False):
    vals = []
    rank = jnp.full(x.shape, float(count), F32) if with_rank else None
    for r in range(count):
        m = jnp.max(x, axis=0, keepdims=True)
        vals.append(m)
        hit = x == m
        if with_rank:
            rank = jnp.where(hit, float(r), rank)
        x = jnp.where(hit, NEG_INF, x)
    return (vals, rank) if with_rank else vals


def _topk_kernel(s_ref, l1_ref, e1_ref, r2_ref, e2_ref):
    tt = s_ref.shape[1]
    ridx = lax.broadcasted_iota(jnp.int32, (8, tt), 0)
    for h in range(PEER_HEADS):
        base = 2 * h * HEAD_DIM
        s1 = s_ref[base:base + HEAD_DIM, :]
        s2 = s_ref[base + HEAD_DIM:base + 2 * HEAD_DIM, :]
        a = _top_values(s1, PEER_TOPK)
        b, rank2 = _top_values(s2, PEER_TOPK, with_rank=True)
        a_lo, a_hi = _stack_rows(a[0:8]), _stack_rows(a[8:16])
        b_lo, b_hi = _stack_rows(b[0:8]), _stack_rows(b[8:16])
        b_mid = jnp.where(ridx >= 5, b_lo, NEG_INF)
        cand = jnp.concatenate(
            [a_lo + b[l] for l in range(5)] + [a_hi + b[0], b_hi + a[0], b_mid + a[0], b_mid + a[1]], axis=0)
        top = _top_values(cand, PEER_TOPK)
        tau = top[PEER_TOPK - 1]
        z = jnp.zeros_like(tau)
        for v in top:
            z = z + jnp.exp(v - top[0])
        limit1 = jnp.zeros_like(s1)
        for bv in b:
            limit1 = limit1 + jnp.where(s1 + bv >= tau, 1.0, 0.0)
        l1_ref[h] = limit1
        e1_ref[h] = jnp.exp(s1 - a[0]) / z
        r2_ref[h * HEAD_DIM:(h + 1) * HEAD_DIM, :] = rank2
        e2_ref[h * HEAD_DIM:(h + 1) * HEAD_DIM, :] = jnp.exp(s2 - b[0])


def _topk_stats(st, tt):
    nq, n = st.shape
    rows = PEER_HEADS * HEAD_DIM
    spec3 = pl.BlockSpec((PEER_HEADS, HEAD_DIM, tt), lambda i: (0, 0, i))
    spec2 = pl.BlockSpec((rows, tt), lambda i: (0, i))
    return pl.pallas_call(
        _topk_kernel,
        out_shape=(jax.ShapeDtypeStruct((PEER_HEADS, HEAD_DIM, n), F32),
                   jax.ShapeDtypeStruct((PEER_HEADS, HEAD_DIM, n), F32),
                   jax.ShapeDtypeStruct((rows, n), F32),
                   jax.ShapeDtypeStruct((rows, n), F32)),
        grid=(n // tt,),
        in_specs=[pl.BlockSpec((nq, tt), lambda i: (0, i))],
        out_specs=(spec3, spec3, spec2, spec2),
        compiler_params=_cparams("arbitrary"),
        name="peer_topk",
    )(st)


PEER_SUB = 256


def _gelu(x):
    return 0.5 * x * (1.0 + jnp.tanh(0.7978845608028654 * (x + 0.044715 * (x * x * x))))


def _peer_kernel(bt_ref, u_ref, vt_ref, r2_ref, e2_ref, l1_ref, e1_ref, h1_ref, mod_ref, gf_ref, y_ref,
                 acc_ref, wa_scr, act_scr, r2_scr, e2_scr, *, groups, rows, ec):
    k = pl.program_id(1)
    t = bt_ref.shape[1]

    @pl.when(k == 0)
    def _():
        acc_ref[...] = jnp.zeros_like(acc_ref)
        r2_scr[...] = r2_ref[...].astype(BF16)
        e2_scr[...] = e2_ref[...].astype(BF16)

    def bcast_bf16(row):
        return jnp.tile(jnp.broadcast_to(row, (16, 128)).astype(BF16), (HEAD_DIM // 16, 1))

    nsub = PEER_SUB // HEAD_DIM
    nchunk = ec // PEER_SUB

    def activations(sc):
        es = slice(sc * PEER_SUB, (sc + 1) * PEER_SUB)
        act_scr[es, :] = _gelu(_dot(u_ref[es, :], bt_ref[...])).astype(BF16)

    def gates(sc):
        for tl in range(t // 128):
            ls = slice(tl * 128, (tl + 1) * 128)
            w = [jnp.zeros((HEAD_DIM, 128), BF16) for _ in range(nsub)]
            for h in range(PEER_HEADS):
                hs = slice(h * HEAD_DIM, (h + 1) * HEAD_DIM)
                rank2 = r2_scr[hs, ls]
                e2 = e2_scr[hs, ls]
                for ii in range(nsub):
                    il = sc * nsub + ii
                    limit1 = bcast_bf16(l1_ref[h, il:il + 1, ls])
                    e1 = bcast_bf16(e1_ref[h, il:il + 1, ls])
                    w[ii] = w[ii] + jnp.minimum(e2 * e1, jnp.maximum(limit1 - rank2, 0))
            for ii in range(nsub):
                rs = slice(sc * PEER_SUB + ii * HEAD_DIM, sc * PEER_SUB + (ii + 1) * HEAD_DIM)
                wa_scr[rs, ls] = w[ii] * act_scr[rs, ls]

    def values(sc):
        es = slice(sc * PEER_SUB, (sc + 1) * PEER_SUB)
        acc_ref[...] += _dot(vt_ref[:, es], wa_scr[es, :])

    activations(0)
    for sc in range(nchunk):
        if sc + 1 < nchunk:
            activations(sc + 1)
        gates(sc)
        values(sc)

    @pl.when(k == pl.num_programs(1) - 1)
    def _():
        out = acc_ref[...].T
        for g in range(groups):
            r = slice(g * rows, (g + 1) * rows)
            h2 = h1_ref[r, :] + mod_ref[g, 5:6, :] * out[r, :]
            y_ref[r, :] = _rms(h2) * gf_ref[...]


def _peer(bt, u, vt, l1, e1, r2, e2, h1, mod, gf, rows_per_batch, t, ec):
    d, n = bt.shape
    ne = u.shape[0]
    nr = r2.shape[0]
    groups, mod_spec = _mod_blockspec(t, rows_per_batch, d, 2)
    spec_i = pl.BlockSpec((PEER_HEADS, ec // HEAD_DIM, t), lambda i, k: (0, k, i))
    spec_j = pl.BlockSpec((nr, t), lambda i, k: (0, i))
    return pl.pallas_call(
        functools.partial(_peer_kernel, groups=groups, rows=t // groups, ec=ec),
        out_shape=jax.ShapeDtypeStruct((n, d), F32),
        grid=(n // t, ne // ec),
        in_specs=[pl.BlockSpec((d, t), lambda i, k: (0, i)),
                  pl.BlockSpec((ec, d), lambda i, k: (k, 0)),
                  pl.BlockSpec((d, ec), lambda i, k: (0, k)),
                  spec_j, spec_j, spec_i, spec_i,
                  pl.BlockSpec((t, d), lambda i, k: (i, 0)),
                  mod_spec,
                  pl.BlockSpec((1, d), lambda i, k: (0, 0))],
        out_specs=pl.BlockSpec((t, d), lambda i, k: (i, 0)),
        scratch_shapes=[pltpu.VMEM((d, t), F32), pltpu.VMEM((ec, t), BF16), pltpu.VMEM((ec, t), BF16),
                        pltpu.VMEM((nr, t), BF16), pltpu.VMEM((nr, t), BF16)],
        compiler_params=_cparams("arbitrary", "arbitrary"),
        name="peer_dense",
    )(bt, u, vt, r2, e2, l1, e1, h1, mod, gf)


def _rope_tables(pos):
    half = HEAD_DIM // 2
    inv_freq = ROPE_BASE ** (-jnp.arange(half, dtype=F32) / half)
    ang = pos.astype(F32)[:, None] * inv_freq[None, :]
    cos, sin = jnp.cos(ang), jnp.sin(ang)
    return jnp.concatenate([cos, cos], axis=1), jnp.concatenate([-sin, sin], axis=1)


def _tile(n, pref):
    t = min(n, pref)
    assert n % t == 0
    return t


def _run_group(x, mod, pos, s0, k_past, v_past, params):
    (g1, w_in, ret_g, w_out, g2, wq_t, keys, u, vt, gf) = params
    nbatch, seq, d = x.shape
    n = nbatch * seq
    x2 = x.reshape(n, d)

    tm = _tile(n, 256)
    reps = max(1, tm // seq)
    cos_t, sin_t = _rope_tables(pos)
    cos_t, sin_t = jnp.tile(cos_t, (reps, 1)), jnp.tile(sin_t, (reps, 1))
    rq, rk, rv, rg, sq, sk, sv = _inproj(x2, mod, g1, w_in, cos_t, sin_t, seq, tm)

    chunk = min(seq, RET_CHUNK)
    ret, s_new = _retention(rq, rk, rv, rg, s0, ret_g, nbatch, seq, chunk)

    if k_past is None:
        so = _stick_breaking_prompt(sq, sk, sv, nbatch, seq)
    else:
        past_len = k_past.shape[1]
        so = _stick_breaking_sample(sq, sk, sv, k_past.reshape(nbatch, past_len, MIX_W),
                                    v_past.reshape(nbatch, past_len, MIX_W), nbatch, seq)

    tp = _tile(n, 512)
    h1, bt, st = _post(x2, ret, so, mod, w_out, g2, wq_t, keys, seq, tp)
    l1, e1, r2, e2 = _topk_stats(st, _tile(n, 256))
    y = _peer(bt, u, vt, l1, e1, r2, e2, h1, mod, gf, seq, tp, 1024)
    return (y.reshape(nbatch, seq, d),
            s_new.reshape(1, nbatch, RET_HEADS, HEAD_DIM, HEAD_DIM),
            sk.reshape(1, nbatch, seq, SB_HEADS, HEAD_DIM),
            sv.reshape(1, nbatch, seq, SB_HEADS, HEAD_DIM))


def kernel(x_prompt, x_sample, c_prompt, c_sample, state_ret, cache_sb_k, cache_sb_v, w_ada, b_ada, norm1_g,
           w_in, ret_norm_g, w_out, norm2_g, peer_w_query, peer_sub_keys, peer_u, peer_v, final_norm_g):
    assert w_ada.shape[0] == 1, "single-layer step"
    bp, seq_p, d = x_prompt.shape
    bs, seq_s, _ = x_sample.shape
    past_len = cache_sb_k.shape[2]

    mod = _adaln(jnp.concatenate([c_prompt, c_sample], axis=0), w_ada[0], b_ada[0]).reshape(bp + bs, 6, d)
    params = (norm1_g[0].reshape(1, d), w_in[0].astype(BF16), ret_norm_g[0].reshape(1, MIX_W),
              w_out[0].astype(BF16), norm2_g[0].reshape(1, d), peer_w_query[0].T.astype(BF16),
              peer_sub_keys[0].astype(BF16), peer_u[0].astype(BF16), peer_v[0].T.astype(BF16),
              final_norm_g.reshape(1, d))

    zeros_state = jnp.zeros((bp, RET_HEADS, HEAD_DIM, HEAD_DIM), F32)
    y_p, s_p, k_p, v_p = _run_group(x_prompt, mod[:bp], jnp.arange(seq_p), zeros_state, None, None, params)
    y_s, s_s, k_s, v_s = _run_group(x_sample, mod[bp:], past_len + jnp.arange(seq_s), state_ret[0],
                                    cache_sb_k[0], cache_sb_v[0], params)
    return (y_p, y_s, s_p, k_p, v_p, s_s, k_s, v_s)
```

```python
import functools

import jax
import jax.numpy as jnp
from jax import lax
from jax.experimental import pallas as pl
from jax.experimental.pallas import tpu as pltpu

F32 = jnp.float32
BF16 = jnp.bfloat16

NORM_EPS = 1e-6
ROPE_BASE = 10000.0
HEAD_DIM = 128
RET_HEADS = 4
SB_HEADS = 4
MIX_W = RET_HEADS * HEAD_DIM
PEER_HEADS = 8
PEER_TOPK = 16
SB_QBLOCK = 128
RET_CHUNK = 128
VMEM_LIMIT = 56 * 1024 * 1024
NEG_INF = float("-inf")


def _cparams(*sem):
    return pltpu.CompilerParams(dimension_semantics=sem, vmem_limit_bytes=VMEM_LIMIT)


def _dot(a, b):
    return jnp.dot(a, b, preferred_element_type=F32)


def _dot_nt(a, b):
    return lax.dot_general(a, b, (((1,), (1,)), ((), ())), preferred_element_type=F32)


def _split_bf16(x):
    hi = x.astype(BF16)
    lo = (x - hi.astype(F32)).astype(BF16)
    return hi, lo


def _rms(x):
    return x * lax.rsqrt(jnp.mean(x * x, axis=-1, keepdims=True) + NORM_EPS)


def _silu(x):
    return x / (1.0 + jnp.exp(-x))


def _adaln_kernel(c_ref, w_ref, b_ref, o_ref):
    s = _silu(c_ref[...])
    s_hi, s_lo = _split_bf16(s)
    w_hi, w_lo = _split_bf16(w_ref[...])
    o_ref[...] = _dot(s_hi, w_hi) + _dot(s_hi, w_lo) + _dot(s_lo, w_hi) + b_ref[...]


def _adaln(c, w_ada, b_ada):
    nb, d = c.shape
    n = w_ada.shape[1]
    tn = 1024
    return pl.pallas_call(
        _adaln_kernel,
        out_shape=jax.ShapeDtypeStruct((nb, n), F32),
        grid=(n // tn,),
        in_specs=[pl.BlockSpec((nb, d), lambda j: (0, 0)),
                  pl.BlockSpec((d, tn), lambda j: (0, j)),
                  pl.BlockSpec((1, tn), lambda j: (0, j))],
        out_specs=pl.BlockSpec((nb, tn), lambda j: (0, j)),
        compiler_params=_cparams("arbitrary"),
        name="adaln",
    )(c, w_ada, b_ada.reshape(1, n))


def _mod_blockspec(tm, rows_per_batch, d, ngrid):
    groups = max(1, tm // rows_per_batch)
    if groups == 1:
        per = rows_per_batch // tm
        if ngrid == 1:
            return groups, pl.BlockSpec((1, 6, d), lambda i: (i // per, 0, 0))
        return groups, pl.BlockSpec((1, 6, d), lambda i, k: (i // per, 0, 0))
    if ngrid == 1:
        return groups, pl.BlockSpec((groups, 6, d), lambda i: (i, 0, 0))
    return groups, pl.BlockSpec((groups, 6, d), lambda i, k: (i, 0, 0))


def _inproj_kernel(x_ref, mod_ref, g1_ref, w_ref, cos_ref, sin_ref,
                   rq_ref, rk_ref, rv_ref, rg_ref, sq_ref, sk_ref, sv_ref, a_scr, *, groups, rows):
    for g in range(groups):
        r = slice(g * rows, (g + 1) * rows)
        xn = _rms(x_ref[r, :]) * g1_ref[...]
        a = xn * (1.0 + mod_ref[g, 1:2, :]) + mod_ref[g, 0:1, :]
        a_scr[r, :] = a.astype(BF16)
    y = _dot(a_scr[...], w_ref[...])
    cos = cos_ref[...]
    sin = sin_ref[...]
    for h in range(RET_HEADS):
        c = slice(h * HEAD_DIM, (h + 1) * HEAD_DIM)
        yq = y[:, h * HEAD_DIM:(h + 1) * HEAD_DIM]
        yk = y[:, MIX_W + h * HEAD_DIM:MIX_W + (h + 1) * HEAD_DIM]
        rq = (yq * cos + pltpu.roll(yq, HEAD_DIM // 2, 1) * sin) * (HEAD_DIM ** -0.5)
        rk = yk * cos + pltpu.roll(yk, HEAD_DIM // 2, 1) * sin
        rq_ref[:, c] = rq.astype(rq_ref.dtype)
        rk_ref[:, c] = rk.astype(rk_ref.dtype)
    rv_ref[...] = y[:, 2 * MIX_W:3 * MIX_W].astype(rv_ref.dtype)
    rg_ref[...] = y[:, 3 * MIX_W:4 * MIX_W].astype(rg_ref.dtype)
    sq_ref[...] = y[:, 4 * MIX_W:5 * MIX_W].astype(sq_ref.dtype)
    sk_ref[...] = y[:, 5 * MIX_W:6 * MIX_W]
    sv_ref[...] = y[:, 6 * MIX_W:7 * MIX_W]


def _inproj(x, mod, g1, w_in, cos_t, sin_t, rows_per_batch, tm):
    n, d = x.shape
    ncol = w_in.shape[1]
    groups, mod_spec = _mod_blockspec(tm, rows_per_batch, d, 1)
    npos = cos_t.shape[0] // tm
    tok = lambda dt: jax.ShapeDtypeStruct((n, MIX_W), dt)
    tspec = pl.BlockSpec((tm, MIX_W), lambda i: (i, 0))
    return pl.pallas_call(
        functools.partial(_inproj_kernel, groups=groups, rows=tm // groups),
        out_shape=(tok(BF16), tok(BF16), tok(BF16), tok(BF16), tok(BF16), tok(F32), tok(F32)),
        grid=(n // tm,),
        in_specs=[pl.BlockSpec((tm, d), lambda i: (i, 0)),
                  mod_spec,
                  pl.BlockSpec((1, d), lambda i: (0, 0)),
                  pl.BlockSpec((d, ncol), lambda i: (0, 0)),
                  pl.BlockSpec((tm, HEAD_DIM), lambda i: (i % npos, 0)),
                  pl.BlockSpec((tm, HEAD_DIM), lambda i: (i % npos, 0))],
        out_specs=(tspec,) * 7,
        scratch_shapes=[pltpu.VMEM((tm, d), BF16)],
        compiler_params=_cparams("arbitrary"),
        name="inproj",
    )(x, mod, g1, w_in, cos_t, sin_t)


def _retention_kernel(q_ref, k_ref, v_ref, g_ref, s0_ref, intra_ref, qd_ref, kd_ref, cd_ref, gn_ref,
                      o_ref, sout_ref, s_scr):
    c = pl.program_id(1)

    @pl.when(c == 0)
    def _():
        s_scr[...] = s0_ref[0]

    for h in range(RET_HEADS):
        cs = slice(h * HEAD_DIM, (h + 1) * HEAD_DIM)
        q = q_ref[:, cs]
        k = k_ref[:, cs]
        v = v_ref[:, cs]
        state = s_scr[h]
        scores = _dot_nt(q, k) * intra_ref[h]
        o = _dot(scores.astype(BF16), v) + _dot(q, state.astype(BF16)) * qd_ref[h]
        kdt = (k.astype(F32) * kd_ref[h]).T.astype(BF16)
        s_scr[h] = cd_ref[h] * state + _dot(kdt, v)
        cen = o - jnp.mean(o, axis=-1, keepdims=True)
        on = cen * lax.rsqrt(jnp.mean(cen * cen, axis=-1, keepdims=True) + NORM_EPS)
        gate = _silu(g_ref[:, cs].astype(F32))
        o_ref[:, cs] = (on * gn_ref[:, cs] * gate).astype(o_ref.dtype)

    @pl.when(c == pl.num_programs(1) - 1)
    def _():
        sout_ref[0] = s_scr[...]


def _retention_tables(chunk):
    log_g = jnp.log1p(-jnp.exp2(-5.0 - jnp.arange(RET_HEADS, dtype=F32)))
    idx = jnp.arange(chunk, dtype=F32)
    diff = idx[:, None] - idx[None, :]
    causal = diff >= 0
    intra = jnp.where(causal[None], jnp.exp(jnp.where(causal, diff, 0.0)[None] * log_g[:, None, None]), 0.0)
    qd = jnp.exp((idx[None, :] + 1.0) * log_g[:, None])
    kd = jnp.exp((chunk - 1.0 - idx)[None, :] * log_g[:, None])
    cd = jnp.exp(chunk * log_g)
    bc = lambda t: jnp.broadcast_to(t[:, :, None], (RET_HEADS, chunk, HEAD_DIM))
    return intra, bc(qd), bc(kd), jnp.broadcast_to(cd[:, None, None], (RET_HEADS, 1, HEAD_DIM))


def _retention(rq, rk, rv, rg, s0, ret_norm_g, nbatch, seq, chunk):
    n = rq.shape[0]
    nc = seq // chunk
    intra, qd, kd, cd = _retention_tables(chunk)
    tspec = pl.BlockSpec((chunk, MIX_W), lambda b, c: (b * nc + c, 0))
    full = lambda shape: pl.BlockSpec(shape, lambda b, c: (0,) * len(shape))
    sspec = pl.BlockSpec((1, RET_HEADS, HEAD_DIM, HEAD_DIM), lambda b, c: (b, 0, 0, 0))
    return pl.pallas_call(
        _retention_kernel,
        out_shape=(jax.ShapeDtypeStruct((n, MIX_W), BF16),
                   jax.ShapeDtypeStruct((nbatch, RET_HEADS, HEAD_DIM, HEAD_DIM), F32)),
        grid=(nbatch, nc),
        in_specs=[tspec, tspec, tspec, tspec, sspec,
                  full((RET_HEADS, chunk, chunk)), full((RET_HEADS, chunk, HEAD_DIM)),
                  full((RET_HEADS, chunk, HEAD_DIM)), full((RET_HEADS, 1, HEAD_DIM)), full((1, MIX_W))],
        out_specs=(tspec, sspec),
        scratch_shapes=[pltpu.VMEM((RET_HEADS, HEAD_DIM, HEAD_DIM), F32)],
        compiler_params=_cparams("arbitrary", "arbitrary"),
        name="retention",
    )(rq, rk, rv, rg, s0, intra, qd, kd, cd, ret_norm_g)


def _log_sigmoid(z):
    return jnp.minimum(z, 0.0) - jnp.log(1.0 + jnp.exp(-jnp.abs(z)))


SB_TILE = 512


def _sb_scores(q, kblk):
    return _dot_nt(q, kblk) * (HEAD_DIM ** -0.5)


def _sb_cumulate(z, cumw, mask):
    nsub = z.shape[1] // 128
    log_beta = _log_sigmoid(z)
    log_keep = log_beta - z
    if mask is not None:
        log_keep = jnp.where(mask, log_keep, 0.0)
    parts = _split_bf16(log_keep)
    lhs = jnp.concatenate(
        [jnp.concatenate([p[:, c * 128:(c + 1) * 128] for p in parts], axis=1) for c in range(nsub)], axis=0)
    return log_beta, _dot(lhs, cumw)


def _sb_weights(log_beta, cw, vblk, carry, mask):
    nsub = vblk.shape[0] // 128
    qb = cw.shape[0] // nsub
    a = [None] * nsub
    for c in reversed(range(nsub)):
        a[c] = jnp.exp(log_beta[:, c * 128:(c + 1) * 128] + (cw[c * qb:(c + 1) * qb, 0:128] + carry))
        carry = carry + cw[c * qb:(c + 1) * qb, 128:256]
    a = jnp.concatenate(a, axis=1)
    if mask is not None:
        a = jnp.where(mask, a, 0.0)
    return _dot(a.astype(BF16), vblk), carry


def _sb_tiles(q_ref, k_of, v_of, cumw_ref, carry_of, mask):
    heads = _sb_heads()
    z, cw, res = {}, {}, {}
    for step in range(len(heads) + 2):
        if step < len(heads):
            z[step] = _sb_scores(q_ref[:, heads[step]], k_of(heads[step]))
        if 0 <= step - 1 < len(heads):
            cw[step - 1] = _sb_cumulate(z.pop(step - 1), cumw_ref[...], mask)
        if 0 <= step - 2 < len(heads):
            hs = heads[step - 2]
            res[step - 2] = _sb_weights(*cw.pop(step - 2), v_of(hs), carry_of(hs), mask)
    return [res[h] for h in range(len(heads))]


def _sb_heads():
    return [slice(h * HEAD_DIM, (h + 1) * HEAD_DIM) for h in range(SB_HEADS)]


def _sb_past_tiles(q_ref, kp_ref, vp_ref, cumw_ref, out_scr, carry_scr, ntiles):
    def body(it, _):
        start = pl.multiple_of((ntiles - 1 - it) * SB_TILE, SB_TILE)
        res = _sb_tiles(q_ref, lambda hs: kp_ref[0, pl.ds(start, SB_TILE), hs].astype(BF16),
                        lambda hs: vp_ref[0, pl.ds(start, SB_TILE), hs].astype(BF16),
                        cumw_ref, lambda hs: carry_scr[:, hs], None)
        for hs, (o, carry) in zip(_sb_heads(), res):
            out_scr[:, hs] += o
            carry_scr[:, hs] = carry
        return 0

    lax.fori_loop(0, ntiles, body, 0)


def _sb_prompt_kernel(q_ref, kp_ref, vp_ref, cumw_ref, o_ref, out_scr, carry_scr):
    i = pl.program_id(1)
    qb = q_ref.shape[0]
    last = (i * qb) // SB_TILE
    start = pl.multiple_of(last * SB_TILE, SB_TILE)
    q_pos = i * qb + lax.broadcasted_iota(jnp.int32, (qb, SB_TILE), 0)
    k_pos = start + lax.broadcasted_iota(jnp.int32, (qb, SB_TILE), 1)
    res = _sb_tiles(q_ref, lambda hs: kp_ref[0, pl.ds(start, SB_TILE), hs].astype(BF16),
                    lambda hs: vp_ref[0, pl.ds(start, SB_TILE), hs].astype(BF16),
                    cumw_ref, lambda hs: jnp.zeros((qb, 128), F32), k_pos < q_pos)
    for hs, (o, carry) in zip(_sb_heads(), res):
        out_scr[:, hs] = o
        carry_scr[:, hs] = carry
    _sb_past_tiles(q_ref, kp_ref, vp_ref, cumw_ref, out_scr, carry_scr, last)
    o_ref[...] = out_scr[...].astype(o_ref.dtype)


def _sb_sample_kernel(q_ref, kn_ref, vn_ref, kc_hbm, vc_hbm, cumw_ref, o_ref,
                      kpad, vpad, out_scr, carry_scr, kbuf, vbuf, sem):
    b = pl.program_id(0)
    qb = q_ref.shape[0]
    ntiles = kc_hbm.shape[2] // SB_TILE
    head_of = {hs.start: h for h, hs in enumerate(_sb_heads())}

    def tile_copies(it, slot):
        start = pl.multiple_of((ntiles - 1 - it) * SB_TILE, SB_TILE)
        cps = []
        for h in range(SB_HEADS):
            cps.append(pltpu.make_async_copy(kc_hbm.at[0, b, pl.ds(start, SB_TILE), h, :],
                                             kbuf.at[slot, h], sem.at[0, slot, h]))
            cps.append(pltpu.make_async_copy(vc_hbm.at[0, b, pl.ds(start, SB_TILE), h, :],
                                             vbuf.at[slot, h], sem.at[1, slot, h]))
        return cps

    for cp in tile_copies(0, 0):
        cp.start()

    kpad[...] = jnp.zeros_like(kpad)
    vpad[...] = jnp.zeros_like(vpad)
    kpad[0:qb, :] = kn_ref[...]
    vpad[0:qb, :] = vn_ref[...]
    tq = lax.broadcasted_iota(jnp.int32, (qb, 128), 0)
    ts = lax.broadcasted_iota(jnp.int32, (qb, 128), 1)
    res = _sb_tiles(q_ref, lambda hs: kpad[:, hs].astype(BF16), lambda hs: vpad[:, hs].astype(BF16),
                    cumw_ref, lambda hs: jnp.zeros((qb, 128), F32), ts < tq)
    for hs, (o, carry) in zip(_sb_heads(), res):
        out_scr[:, hs] = o
        carry_scr[:, hs] = carry

    def body(it, _):
        slot = it % 2

        @pl.when(it + 1 < ntiles)
        def _():
            for cp in tile_copies(it + 1, 1 - slot):
                cp.start()

        for cp in tile_copies(it, slot):
            cp.wait()
        res = _sb_tiles(q_ref, lambda hs: kbuf[slot, head_of[hs.start]].astype(BF16),
                        lambda hs: vbuf[slot, head_of[hs.start]].astype(BF16),
                        cumw_ref, lambda hs: carry_scr[:, hs], None)
        for hs, (o, carry) in zip(_sb_heads(), res):
            out_scr[:, hs] += o
            carry_scr[:, hs] = carry
        return 0

    lax.fori_loop(0, ntiles, body, 0)
    o_ref[...] = out_scr[...].astype(o_ref.dtype)


def _cumw():
    n = 128
    later = lax.broadcasted_iota(jnp.int32, (n, n), 0) > lax.broadcasted_iota(jnp.int32, (n, n), 1)
    half = jnp.concatenate([later.astype(F32), jnp.ones((n, n), F32)], axis=1)
    return jnp.concatenate([half, half], axis=0).astype(BF16)


def _stick_breaking_prompt(sq, sk, sv, nbatch, seq):
    n = sq.shape[0]
    qb = SB_QBLOCK
    nq = seq // qb
    assert seq % SB_TILE == 0
    tspec = pl.BlockSpec((qb, MIX_W), lambda b, i: (b * nq + i, 0))
    pspec = pl.BlockSpec((1, seq, MIX_W), lambda b, i: (b, 0, 0))
    return pl.pallas_call(
        _sb_prompt_kernel,
        out_shape=jax.ShapeDtypeStruct((n, MIX_W), BF16),
        grid=(nbatch, nq),
        in_specs=[tspec, pspec, pspec, pl.BlockSpec((256, 256), lambda b, i: (0, 0))],
        out_specs=tspec,
        scratch_shapes=[pltpu.VMEM((qb, MIX_W), F32), pltpu.VMEM((qb, MIX_W), F32)],
        compiler_params=_cparams("arbitrary", "arbitrary"),
        name="stick_breaking",
    )(sq, sk.reshape(nbatch, seq, MIX_W), sv.reshape(nbatch, seq, MIX_W), _cumw())


def _stick_breaking_sample(sq, sk, sv, k_cache, v_cache, nbatch, seq):
    n = sq.shape[0]
    past_len = k_cache.shape[2]
    assert seq <= 128 and past_len % SB_TILE == 0 and past_len >= SB_TILE
    tspec = pl.BlockSpec((seq, MIX_W), lambda b: (b, 0))
    hbm = pl.BlockSpec(memory_space=pl.ANY)
    return pl.pallas_call(
        _sb_sample_kernel,
        out_shape=jax.ShapeDtypeStruct((n, MIX_W), BF16),
        grid=(nbatch,),
        in_specs=[tspec, tspec, tspec, hbm, hbm, pl.BlockSpec((256, 256), lambda b: (0, 0))],
        out_specs=tspec,
        scratch_shapes=[pltpu.VMEM((128, MIX_W), F32), pltpu.VMEM((128, MIX_W), F32),
                        pltpu.VMEM((seq, MIX_W), F32), pltpu.VMEM((seq, MIX_W), F32),
                        pltpu.VMEM((2, SB_HEADS, SB_TILE, HEAD_DIM), F32),
                        pltpu.VMEM((2, SB_HEADS, SB_TILE, HEAD_DIM), F32),
                        pltpu.SemaphoreType.DMA((2, 2, SB_HEADS))],
        compiler_params=_cparams("arbitrary"),
        name="stick_breaking_cached",
    )(sq, sk, sv, k_cache, v_cache, _cumw())


def _post_kernel(x_ref, ret_ref, so_ref, mod_ref, wo_ref, g2_ref, wqt_ref, keys_ref,
                 h1_ref, bt_ref, st_ref, b_scr, *, groups, rows):
    attn = _dot(ret_ref[...], wo_ref[0:MIX_W, :]) + _dot(so_ref[...], wo_ref[MIX_W:2 * MIX_W, :])
    for g in range(groups):
        r = slice(g * rows, (g + 1) * rows)
        h1 = x_ref[r, :] + mod_ref[g, 2:3, :] * attn[r, :]
        h1_ref[r, :] = h1
        b_scr[r, :] = _rms(h1) * g2_ref[...] * (1.0 + mod_ref[g, 4:5, :]) + mod_ref[g, 3:4, :]
    bt = b_scr[...].T.astype(BF16)
    bt_ref[...] = bt
    qt = _dot(wqt_ref[...], bt).astype(BF16)
    for hp in range(2 * PEER_HEADS):
        rs = slice(hp * HEAD_DIM, (hp + 1) * HEAD_DIM)
        st_ref[rs, :] = _dot(keys_ref[hp % 2, hp // 2], qt[rs, :])


def _post(x, ret, so, mod, w_out, g2, wq_t, keys, rows_per_batch, tm):
    n, d = x.shape
    nq = wq_t.shape[0]
    groups, mod_spec = _mod_blockspec(tm, rows_per_batch, d, 1)
    return pl.pallas_call(
        functools.partial(_post_kernel, groups=groups, rows=tm // groups),
        out_shape=(jax.ShapeDtypeStruct((n, d), F32),
                   jax.ShapeDtypeStruct((d, n), BF16),
                   jax.ShapeDtypeStruct((nq, n), F32)),
        grid=(n // tm,),
        in_specs=[pl.BlockSpec((tm, d), lambda i: (i, 0)),
                  pl.BlockSpec((tm, MIX_W), lambda i: (i, 0)),
                  pl.BlockSpec((tm, MIX_W), lambda i: (i, 0)),
                  mod_spec,
                  pl.BlockSpec((2 * MIX_W, d), lambda i: (0, 0)),
                  pl.BlockSpec((1, d), lambda i: (0, 0)),
                  pl.BlockSpec((nq, d), lambda i: (0, 0)),
                  pl.BlockSpec(keys.shape, lambda i: (0, 0, 0, 0))],
        out_specs=(pl.BlockSpec((tm, d), lambda i: (i, 0)),
                   pl.BlockSpec((d, tm), lambda i: (0, i)),
                   pl.BlockSpec((nq, tm), lambda i: (0, i))),
        scratch_shapes=[pltpu.VMEM((tm, d), F32)],
        compiler_params=_cparams("arbitrary"),
        name="post_mix",
    )(x, ret, so, mod, w_out, g2, wq_t, keys)


def _stack_rows(rows):
    t = rows[0].shape[1]
    ridx = lax.broadcasted_iota(jnp.int32, (8, t), 0)
    out = jnp.broadcast_to(rows[0], (8, t))
    for r in range(1, len(rows)):
        out = jnp.where(ridx == r, rows[r], out)
    return out


def _top_values(x, count, with_rank=False):
    vals = []
    rank = jnp.full(x.shape, float(count), F32) if with_rank else None
    for r in range(count):
        m = jnp.max(x, axis=0, keepdims=True)
        vals.append(m)
        hit = x == m
        if with_rank:
            rank = jnp.where(hit, float(r), rank)
        x = jnp.where(hit, NEG_INF, x)
    return (vals, rank) if with_rank else vals


def _topk_kernel(s_ref, l1_ref, e1_ref, r2_ref, e2_ref):
    tt = s_ref.shape[1]
    ridx = lax.broadcasted_iota(jnp.int32, (8, tt), 0)
    for h in range(PEER_HEADS):
        base = 2 * h * HEAD_DIM
        s1 = s_ref[base:base + HEAD_DIM, :]
        s2 = s_ref[base + HEAD_DIM:base + 2 * HEAD_DIM, :]
        a = _top_values(s1, PEER_TOPK)
        b, rank2 = _top_values(s2, PEER_TOPK, with_rank=True)
        a_lo, a_hi = _stack_rows(a[0:8]), _stack_rows(a[8:16])
        b_lo, b_hi = _stack_rows(b[0:8]), _stack_rows(b[8:16])
        b_mid = jnp.where(ridx >= 5, b_lo, NEG_INF)
        cand = jnp.concatenate(
            [a_lo + b[l] for l in range(5)] + [a_hi + b[0], b_hi + a[0], b_mid + a[0], b_mid + a[1]], axis=0)
        top = _top_values(cand, PEER_TOPK)
        tau = top[PEER_TOPK - 1]
        z = jnp.zeros_like(tau)
        for v in top:
            z = z + jnp.exp(v - top[0])
        limit1 = jnp.zeros_like(s1)
        for bv in b:
            limit1 = limit1 + jnp.where(s1 + bv >= tau, 1.0, 0.0)
        l1_ref[h] = limit1
        e1_ref[h] = jnp.exp(s1 - a[0]) / z
        r2_ref[h * HEAD_DIM:(h + 1) * HEAD_DIM, :] = rank2
        e2_ref[h * HEAD_DIM:(h + 1) * HEAD_DIM, :] = jnp.exp(s2 - b[0])


def _topk_stats(st, tt):
    nq, n = st.shape
    rows = PEER_HEADS * HEAD_DIM
    spec3 = pl.BlockSpec((PEER_HEADS, HEAD_DIM, tt), lambda i: (0, 0, i))
    spec2 = pl.BlockSpec((rows, tt), lambda i: (0, i))
    return pl.pallas_call(
        _topk_kernel,
        out_shape=(jax.ShapeDtypeStruct((PEER_HEADS, HEAD_DIM, n), F32),
                   jax.ShapeDtypeStruct((PEER_HEADS, HEAD_DIM, n), F32),
                   jax.ShapeDtypeStruct((rows, n), F32),
                   jax.ShapeDtypeStruct((rows, n), F32)),
        grid=(n // tt,),
        in_specs=[pl.BlockSpec((nq, tt), lambda i: (0, i))],
        out_specs=(spec3, spec3, spec2, spec2),
        compiler_params=_cparams("arbitrary"),
        name="peer_topk",
    )(st)


PEER_SUB = 256


def _gelu(x):
    half = 0.5 * x
    inner = x * (0.7978845608028654 + (0.7978845608028654 * 0.044715) * (x * x))
    return half + half * jnp.tanh(inner)


def _peer_kernel(bt_ref, u_ref, vt_ref, r2_ref, e2_ref, l1_ref, e1_ref, h1_ref, mod_ref, gf_ref, y_ref,
                 acc_ref, wa_scr, act_scr, r2_scr, e2_scr, *, groups, rows, ec):
    k = pl.program_id(1)
    t = bt_ref.shape[1]

    @pl.when(k == 0)
    def _():
        acc_ref[...] = jnp.zeros_like(acc_ref)
        r2_scr[...] = r2_ref[...].astype(BF16)
        e2_scr[...] = e2_ref[...].astype(BF16)

    def bcast_bf16(row):
        return jnp.tile(jnp.broadcast_to(row, (16, 128)).astype(BF16), (HEAD_DIM // 16, 1))

    nsub = PEER_SUB // HEAD_DIM
    nchunk = ec // PEER_SUB

    def activations(sc):
        es = slice(sc * PEER_SUB, (sc + 1) * PEER_SUB)
        act_scr[es, :] = _gelu(_dot(u_ref[es, :], bt_ref[...])).astype(BF16)

    def gates(sc):
        for tl in range(t // 128):
            ls = slice(tl * 128, (tl + 1) * 128)
            w = [jnp.zeros((HEAD_DIM, 128), BF16) for _ in range(nsub)]
            for h in range(PEER_HEADS):
                hs = slice(h * HEAD_DIM, (h + 1) * HEAD_DIM)
                rank2 = r2_scr[hs, ls]
                e2 = e2_scr[hs, ls]
                for ii in range(nsub):
                    il = sc * nsub + ii
                    limit1 = bcast_bf16(l1_ref[h, il:il + 1, ls])
                    e1 = bcast_bf16(e1_ref[h, il:il + 1, ls])
                    w[ii] = w[ii] + jnp.minimum(e2 * e1, jnp.maximum(limit1 - rank2, 0))
            for ii in range(nsub):
                rs = slice(sc * PEER_SUB + ii * HEAD_DIM, sc * PEER_SUB + (ii + 1) * HEAD_DIM)
                wa_scr[rs, ls] = w[ii] * act_scr[rs, ls]

    def values(sc):
        es = slice(sc * PEER_SUB, (sc + 1) * PEER_SUB)
        acc_ref[...] += _dot(vt_ref[:, es], wa_scr[es, :])

    activations(0)
    for sc in range(nchunk):
        if sc + 1 < nchunk:
            activations(sc + 1)
        gates(sc)
        values(sc)

    @pl.when(k == pl.num_programs(1) - 1)
    def _():
        out = acc_ref[...].T
        for g in range(groups):
            r = slice(g * rows, (g + 1) * rows)
            h2 = h1_ref[r, :] + mod_ref[g, 5:6, :] * out[r, :]
            y_ref[r, :] = _rms(h2) * gf_ref[...]


def _peer(bt, u, vt, l1, e1, r2, e2, h1, mod, gf, rows_per_batch, t, ec):
    d, n = bt.shape
    ne = u.shape[0]
    nr = r2.shape[0]
    groups, mod_spec = _mod_blockspec(t, rows_per_batch, d, 2)
    spec_i = pl.BlockSpec((PEER_HEADS, ec // HEAD_DIM, t), lambda i, k: (0, k, i))
    spec_j = pl.BlockSpec((nr, t), lambda i, k: (0, i))
    return pl.pallas_call(
        functools.partial(_peer_kernel, groups=groups, rows=t // groups, ec=ec),
        out_shape=jax.ShapeDtypeStruct((n, d), F32),
        grid=(n // t, ne // ec),
        in_specs=[pl.BlockSpec((d, t), lambda i, k: (0, i)),
                  pl.BlockSpec((ec, d), lambda i, k: (k, 0)),
                  pl.BlockSpec((d, ec), lambda i, k: (0, k)),
                  spec_j, spec_j, spec_i, spec_i,
                  pl.BlockSpec((t, d), lambda i, k: (i, 0)),
                  mod_spec,
                  pl.BlockSpec((1, d), lambda i, k: (0, 0))],
        out_specs=pl.BlockSpec((t, d), lambda i, k: (i, 0)),
        scratch_shapes=[pltpu.VMEM((d, t), F32), pltpu.VMEM((ec, t), BF16), pltpu.VMEM((ec, t), BF16),
                        pltpu.VMEM((nr, t), BF16), pltpu.VMEM((nr, t), BF16)],
        compiler_params=_cparams("arbitrary", "arbitrary"),
        name="peer_dense",
    )(bt, u, vt, r2, e2, l1, e1, h1, mod, gf)


def _rope_tables(pos):
    half = HEAD_DIM // 2
    inv_freq = ROPE_BASE ** (-jnp.arange(half, dtype=F32) / half)
    ang = pos.astype(F32)[:, None] * inv_freq[None, :]
    cos, sin = jnp.cos(ang), jnp.sin(ang)
    return jnp.concatenate([cos, cos], axis=1), jnp.concatenate([-sin, sin], axis=1)


def _tile(n, pref):
    t = min(n, pref)
    assert n % t == 0
    return t


def _run_group(x, mod, pos, s0, k_past, v_past, params):
    (g1, w_in, ret_g, w_out, g2, wq_t, keys, u, vt, gf) = params
    nbatch, seq, d = x.shape
    n = nbatch * seq
    x2 = x.reshape(n, d)

    tm = _tile(n, 256)
    reps = max(1, tm // seq)
    cos_t, sin_t = _rope_tables(pos)
    cos_t, sin_t = jnp.tile(cos_t, (reps, 1)), jnp.tile(sin_t, (reps, 1))
    rq, rk, rv, rg, sq, sk, sv = _inproj(x2, mod, g1, w_in, cos_t, sin_t, seq, tm)

    chunk = min(seq, RET_CHUNK)
    ret, s_new = _retention(rq, rk, rv, rg, s0, ret_g, nbatch, seq, chunk)

    if k_past is None:
        so = _stick_breaking_prompt(sq, sk, sv, nbatch, seq)
    else:
        so = _stick_breaking_sample(sq, sk, sv, k_past, v_past, nbatch, seq)

    tp = _tile(n, 512)
    h1, bt, st = _post(x2, ret, so, mod, w_out, g2, wq_t, keys, seq, tp)
    l1, e1, r2, e2 = _topk_stats(st, _tile(n, 256))
    y = _peer(bt, u, vt, l1, e1, r2, e2, h1, mod, gf, seq, tp, 1024)
    return (y.reshape(nbatch, seq, d),
            s_new.reshape(1, nbatch, RET_HEADS, HEAD_DIM, HEAD_DIM),
            sk.reshape(1, nbatch, seq, SB_HEADS, HEAD_DIM),
            sv.reshape(1, nbatch, seq, SB_HEADS, HEAD_DIM))


def kernel(x_prompt, x_sample, c_prompt, c_sample, state_ret, cache_sb_k, cache_sb_v, w_ada, b_ada, norm1_g,
           w_in, ret_norm_g, w_out, norm2_g, peer_w_query, peer_sub_keys, peer_u, peer_v, final_norm_g):
    assert w_ada.shape[0] == 1, "single-layer step"
    bp, seq_p, d = x_prompt.shape
    bs, seq_s, _ = x_sample.shape
    past_len = cache_sb_k.shape[2]

    mod = _adaln(jnp.concatenate([c_prompt, c_sample], axis=0), w_ada[0], b_ada[0]).reshape(bp + bs, 6, d)
    params = (norm1_g[0].reshape(1, d), w_in[0].astype(BF16), ret_norm_g[0].reshape(1, MIX_W),
              w_out[0].astype(BF16), norm2_g[0].reshape(1, d), peer_w_query[0].T.astype(BF16),
              peer_sub_keys[0].astype(BF16), peer_u[0].astype(BF16), peer_v[0].T.astype(BF16),
              final_norm_g.reshape(1, d))

    zeros_state = jnp.zeros((bp, RET_HEADS, HEAD_DIM, HEAD_DIM), F32)
    y_p, s_p, k_p, v_p = _run_group(x_prompt, mod[:bp], jnp.arange(seq_p), zeros_state, None, None, params)
    y_s, s_s, k_s, v_s = _run_group(x_sample, mod[bp:], past_len + jnp.arange(seq_s), state_ret[0],
                                    cache_sb_k, cache_sb_v, params)
    return (y_p, y_s, s_p, k_p, v_p, s_s, k_s, v_s)
```

```python
import functools

import jax
import jax.numpy as jnp
from jax import lax
from jax.experimental import pallas as pl
from jax.experimental.pallas import tpu as pltpu

F32 = jnp.float32
BF16 = jnp.bfloat16

NORM_EPS = 1e-6
ROPE_BASE = 10000.0
HEAD_DIM = 128
RET_HEADS = 4
SB_HEADS = 4
MIX_W = RET_HEADS * HEAD_DIM
PEER_HEADS = 8
PEER_TOPK = 16
SB_QBLOCK = 512
RET_CHUNK = 512
VMEM_LIMIT = 56 * 1024 * 1024
NEG_INF = float("-inf")


def _cparams(*sem):
    return pltpu.CompilerParams(dimension_semantics=sem, vmem_limit_bytes=VMEM_LIMIT)


def _dot(a, b):
    return jnp.dot(a, b, preferred_element_type=F32)


def _dot_nt(a, b):
    return lax.dot_general(a, b, (((1,), (1,)), ((), ())), preferred_element_type=F32)


def _split_bf16(x):
    hi = x.astype(BF16)
    lo = (x - hi.astype(F32)).astype(BF16)
    return hi, lo


def _rms(x):
    return x * lax.rsqrt(jnp.mean(x * x, axis=-1, keepdims=True) + NORM_EPS)


def _silu(x):
    return x / (1.0 + jnp.exp(-x))


def _adaln_kernel(c_ref, w_ref, b_ref, o_ref):
    s = _silu(c_ref[...])
    s_hi, s_lo = _split_bf16(s)
    w_hi, w_lo = _split_bf16(w_ref[...])
    o_ref[...] = _dot(s_hi, w_hi) + _dot(s_hi, w_lo) + _dot(s_lo, w_hi) + b_ref[...]


def _adaln(c, w_ada, b_ada):
    nb, d = c.shape
    n = w_ada.shape[1]
    tn = 1024
    return pl.pallas_call(
        _adaln_kernel,
        out_shape=jax.ShapeDtypeStruct((nb, n), F32),
        grid=(n // tn,),
        in_specs=[pl.BlockSpec((nb, d), lambda j: (0, 0)),
                  pl.BlockSpec((d, tn), lambda j: (0, j)),
                  pl.BlockSpec((1, tn), lambda j: (0, j))],
        out_specs=pl.BlockSpec((nb, tn), lambda j: (0, j)),
        compiler_params=_cparams("arbitrary"),
        name="adaln",
    )(c, w_ada, b_ada.reshape(1, n))


def _mod_blockspec(tm, rows_per_batch, d, ngrid):
    groups = max(1, tm // rows_per_batch)
    if groups == 1:
        per = rows_per_batch // tm
        if ngrid == 1:
            return groups, pl.BlockSpec((1, 6, d), lambda i: (i // per, 0, 0))
        return groups, pl.BlockSpec((1, 6, d), lambda i, k: (i // per, 0, 0))
    if ngrid == 1:
        return groups, pl.BlockSpec((groups, 6, d), lambda i: (i, 0, 0))
    return groups, pl.BlockSpec((groups, 6, d), lambda i, k: (i, 0, 0))


def _inproj_kernel(x_ref, mod_ref, g1_ref, w_ref, cos_ref, sin_ref,
                   rq_ref, rk_ref, rv_ref, rg_ref, sq_ref, sk_ref, sv_ref, a_scr, *, groups, rows):
    for g in range(groups):
        r = slice(g * rows, (g + 1) * rows)
        xn = _rms(x_ref[r, :]) * g1_ref[...]
        a = xn * (1.0 + mod_ref[g, 1:2, :]) + mod_ref[g, 0:1, :]
        a_scr[r, :] = a.astype(BF16)
    y = _dot(a_scr[...], w_ref[...])
    cos = cos_ref[...]
    sin = sin_ref[...]
    for h in range(RET_HEADS):
        c = slice(h * HEAD_DIM, (h + 1) * HEAD_DIM)
        yq = y[:, h * HEAD_DIM:(h + 1) * HEAD_DIM]
        yk = y[:, MIX_W + h * HEAD_DIM:MIX_W + (h + 1) * HEAD_DIM]
        rq = (yq * cos + pltpu.roll(yq, HEAD_DIM // 2, 1) * sin) * (HEAD_DIM ** -0.5)
        rk = yk * cos + pltpu.roll(yk, HEAD_DIM // 2, 1) * sin
        rq_ref[:, c] = rq.astype(rq_ref.dtype)
        rk_ref[:, c] = rk.astype(rk_ref.dtype)
    rv_ref[...] = y[:, 2 * MIX_W:3 * MIX_W].astype(rv_ref.dtype)
    rg_ref[...] = y[:, 3 * MIX_W:4 * MIX_W].astype(rg_ref.dtype)
    sq_ref[...] = y[:, 4 * MIX_W:5 * MIX_W].astype(sq_ref.dtype)
    sk_ref[...] = y[:, 5 * MIX_W:6 * MIX_W]
    sv_ref[...] = y[:, 6 * MIX_W:7 * MIX_W]


def _inproj(x, mod, g1, w_in, cos_t, sin_t, rows_per_batch, tm):
    n, d = x.shape
    ncol = w_in.shape[1]
    groups, mod_spec = _mod_blockspec(tm, rows_per_batch, d, 1)
    npos = cos_t.shape[0] // tm
    tok = lambda dt: jax.ShapeDtypeStruct((n, MIX_W), dt)
    tspec = pl.BlockSpec((tm, MIX_W), lambda i: (i, 0))
    return pl.pallas_call(
        functools.partial(_inproj_kernel, groups=groups, rows=tm // groups),
        out_shape=(tok(BF16), tok(BF16), tok(BF16), tok(BF16), tok(BF16), tok(F32), tok(F32)),
        grid=(n // tm,),
        in_specs=[pl.BlockSpec((tm, d), lambda i: (i, 0)),
                  mod_spec,
                  pl.BlockSpec((1, d), lambda i: (0, 0)),
                  pl.BlockSpec((d, ncol), lambda i: (0, 0)),
                  pl.BlockSpec((tm, HEAD_DIM), lambda i: (i % npos, 0)),
                  pl.BlockSpec((tm, HEAD_DIM), lambda i: (i % npos, 0))],
        out_specs=(tspec,) * 7,
        scratch_shapes=[pltpu.VMEM((tm, d), BF16)],
        compiler_params=_cparams("arbitrary"),
        name="inproj",
    )(x, mod, g1, w_in, cos_t, sin_t)


def _retention_kernel(q_ref, k_ref, v_ref, g_ref, s0_ref, intra_ref, qd_ref, kd_ref, cd_ref, gn_ref,
                      o_ref, sout_ref, s_scr):
    c = pl.program_id(1)

    @pl.when(c == 0)
    def _():
        s_scr[...] = s0_ref[0]

    for h in range(RET_HEADS):
        cs = slice(h * HEAD_DIM, (h + 1) * HEAD_DIM)
        q = q_ref[:, cs]
        k = k_ref[:, cs]
        v = v_ref[:, cs]
        state = s_scr[h]
        scores = _dot_nt(q, k) * intra_ref[h]
        o = _dot(scores.astype(BF16), v) + _dot(q, state.astype(BF16)) * qd_ref[h]
        kdt = (k.astype(F32) * kd_ref[h]).T.astype(BF16)
        s_scr[h] = cd_ref[h] * state + _dot(kdt, v)
        cen = o - jnp.mean(o, axis=-1, keepdims=True)
        on = cen * lax.rsqrt(jnp.mean(cen * cen, axis=-1, keepdims=True) + NORM_EPS)
        gate = _silu(g_ref[:, cs].astype(F32))
        o_ref[:, cs] = (on * gn_ref[:, cs] * gate).astype(o_ref.dtype)

    @pl.when(c == pl.num_programs(1) - 1)
    def _():
        sout_ref[0] = s_scr[...]


def _retention_tables(chunk):
    log_g = jnp.log1p(-jnp.exp2(-5.0 - jnp.arange(RET_HEADS, dtype=F32)))
    idx = jnp.arange(chunk, dtype=F32)
    diff = idx[:, None] - idx[None, :]
    causal = diff >= 0
    intra = jnp.where(causal[None], jnp.exp(jnp.where(causal, diff, 0.0)[None] * log_g[:, None, None]), 0.0)
    qd = jnp.exp((idx[None, :] + 1.0) * log_g[:, None])
    kd = jnp.exp((chunk - 1.0 - idx)[None, :] * log_g[:, None])
    cd = jnp.exp(chunk * log_g)
    bc = lambda t: jnp.broadcast_to(t[:, :, None], (RET_HEADS, chunk, HEAD_DIM))
    return intra, bc(qd), bc(kd), jnp.broadcast_to(cd[:, None, None], (RET_HEADS, 1, HEAD_DIM))


def _retention(rq, rk, rv, rg, s0, ret_norm_g, nbatch, seq, chunk):
    n = rq.shape[0]
    nc = seq // chunk
    intra, qd, kd, cd = _retention_tables(chunk)
    tspec = pl.BlockSpec((chunk, MIX_W), lambda b, c: (b * nc + c, 0))
    full = lambda shape: pl.BlockSpec(shape, lambda b, c: (0,) * len(shape))
    sspec = pl.BlockSpec((1, RET_HEADS, HEAD_DIM, HEAD_DIM), lambda b, c: (b, 0, 0, 0))
    return pl.pallas_call(
        _retention_kernel,
        out_shape=(jax.ShapeDtypeStruct((n, MIX_W), BF16),
                   jax.ShapeDtypeStruct((nbatch, RET_HEADS, HEAD_DIM, HEAD_DIM), F32)),
        grid=(nbatch, nc),
        in_specs=[tspec, tspec, tspec, tspec, sspec,
                  full((RET_HEADS, chunk, chunk)), full((RET_HEADS, chunk, HEAD_DIM)),
                  full((RET_HEADS, chunk, HEAD_DIM)), full((RET_HEADS, 1, HEAD_DIM)), full((1, MIX_W))],
        out_specs=(tspec, sspec),
        scratch_shapes=[pltpu.VMEM((RET_HEADS, HEAD_DIM, HEAD_DIM), F32)],
        compiler_params=_cparams("arbitrary", "arbitrary"),
        name="retention",
    )(rq, rk, rv, rg, s0, intra, qd, kd, cd, ret_norm_g)


def _log_sigmoid(z):
    return jnp.minimum(z, 0.0) - jnp.log(1.0 + jnp.exp(-jnp.abs(z)))


SB_TILE = 512
SB_CACHE_TILE = 2048


def _sb_scores(q, kblk):
    return _dot_nt(q, kblk) * (HEAD_DIM ** -0.5)


def _sb_cumulate(z, cumw, mask):
    nsub = z.shape[1] // 128
    log_beta = _log_sigmoid(z)
    log_keep = log_beta - z
    if mask is not None:
        log_keep = jnp.where(mask, log_keep, 0.0)
    parts = _split_bf16(log_keep)
    lhs = jnp.concatenate(
        [jnp.concatenate([p[:, c * 128:(c + 1) * 128] for p in parts], axis=1) for c in range(nsub)], axis=0)
    return log_beta, _dot(lhs, cumw)


def _sb_weights(log_beta, cw, vblk, carry, mask):
    nsub = vblk.shape[0] // 128
    qb = cw.shape[0] // nsub
    a = [None] * nsub
    for c in reversed(range(nsub)):
        a[c] = jnp.exp(log_beta[:, c * 128:(c + 1) * 128] + (cw[c * qb:(c + 1) * qb, 0:128] + carry))
        carry = carry + cw[c * qb:(c + 1) * qb, 128:256]
    a = jnp.concatenate(a, axis=1)
    if mask is not None:
        a = jnp.where(mask, a, 0.0)
    return _dot(a.astype(BF16), vblk), carry


def _sb_tiles(q_ref, k_of, v_of, cumw_ref, carry_of, mask):
    heads = _sb_heads()
    z, cw, res = {}, {}, {}
    for step in range(len(heads) + 2):
        if step < len(heads):
            z[step] = _sb_scores(q_ref[:, heads[step]], k_of(heads[step]))
        if 0 <= step - 1 < len(heads):
            cw[step - 1] = _sb_cumulate(z.pop(step - 1), cumw_ref[...], mask)
        if 0 <= step - 2 < len(heads):
            hs = heads[step - 2]
            res[step - 2] = _sb_weights(*cw.pop(step - 2), v_of(hs), carry_of(hs), mask)
    return [res[h] for h in range(len(heads))]


def _sb_heads():
    return [slice(h * HEAD_DIM, (h + 1) * HEAD_DIM) for h in range(SB_HEADS)]


def _sb_past_tiles(q_ref, kp_ref, vp_ref, cumw_ref, out_scr, carry_scr, ntiles):
    def body(it, _):
        start = pl.multiple_of((ntiles - 1 - it) * SB_TILE, SB_TILE)
        res = _sb_tiles(q_ref, lambda hs: kp_ref[0, pl.ds(start, SB_TILE), hs].astype(BF16),
                        lambda hs: vp_ref[0, pl.ds(start, SB_TILE), hs].astype(BF16),
                        cumw_ref, lambda hs: carry_scr[:, hs], None)
        for hs, (o, carry) in zip(_sb_heads(), res):
            out_scr[:, hs] += o
            carry_scr[:, hs] = carry
        return 0

    lax.fori_loop(0, ntiles, body, 0)


def _sb_prompt_kernel(q_ref, kp_ref, vp_ref, cumw_ref, o_ref, out_scr, carry_scr):
    i = pl.program_id(1)
    qb = q_ref.shape[0]
    last = (i * qb) // SB_TILE
    start = pl.multiple_of(last * SB_TILE, SB_TILE)
    q_pos = i * qb + lax.broadcasted_iota(jnp.int32, (qb, SB_TILE), 0)
    k_pos = start + lax.broadcasted_iota(jnp.int32, (qb, SB_TILE), 1)
    res = _sb_tiles(q_ref, lambda hs: kp_ref[0, pl.ds(start, SB_TILE), hs].astype(BF16),
                    lambda hs: vp_ref[0, pl.ds(start, SB_TILE), hs].astype(BF16),
                    cumw_ref, lambda hs: jnp.zeros((qb, 128), F32), k_pos < q_pos)
    for hs, (o, carry) in zip(_sb_heads(), res):
        out_scr[:, hs] = o
        carry_scr[:, hs] = carry
    _sb_past_tiles(q_ref, kp_ref, vp_ref, cumw_ref, out_scr, carry_scr, last)
    o_ref[...] = out_scr[...].astype(o_ref.dtype)


def _sb_sample_kernel(q_ref, kn_ref, vn_ref, kc_hbm, vc_hbm, cumw_ref, o_ref,
                      kpad, vpad, out_scr, carry_scr, kbuf, vbuf, sem):
    b = pl.program_id(0)
    qb = q_ref.shape[0]
    tile = kbuf.shape[2]
    ntiles = kc_hbm.shape[2] // tile
    head_of = {hs.start: h for h, hs in enumerate(_sb_heads())}

    def tile_copies(it, slot):
        start = pl.multiple_of((ntiles - 1 - it) * tile, tile)
        cps = []
        for h in range(SB_HEADS):
            cps.append(pltpu.make_async_copy(kc_hbm.at[0, b, pl.ds(start, tile), h, :],
                                             kbuf.at[slot, h], sem.at[0, slot, h]))
            cps.append(pltpu.make_async_copy(vc_hbm.at[0, b, pl.ds(start, tile), h, :],
                                             vbuf.at[slot, h], sem.at[1, slot, h]))
        return cps

    for cp in tile_copies(0, 0):
        cp.start()

    kpad[...] = jnp.zeros_like(kpad)
    vpad[...] = jnp.zeros_like(vpad)
    kpad[0:qb, :] = kn_ref[...]
    vpad[0:qb, :] = vn_ref[...]
    tq = lax.broadcasted_iota(jnp.int32, (qb, 128), 0)
    ts = lax.broadcasted_iota(jnp.int32, (qb, 128), 1)
    res = _sb_tiles(q_ref, lambda hs: kpad[:, hs].astype(BF16), lambda hs: vpad[:, hs].astype(BF16),
                    cumw_ref, lambda hs: jnp.zeros((qb, 128), F32), ts < tq)
    for hs, (o, carry) in zip(_sb_heads(), res):
        out_scr[:, hs] = o
        carry_scr[:, hs] = carry

    def body(it, _):
        slot = it % 2

        @pl.when(it + 1 < ntiles)
        def _():
            for cp in tile_copies(it + 1, 1 - slot):
                cp.start()

        for cp in tile_copies(it, slot):
            cp.wait()
        res = _sb_tiles(q_ref, lambda hs: kbuf[slot, head_of[hs.start]].astype(BF16),
                        lambda hs: vbuf[slot, head_of[hs.start]].astype(BF16),
                        cumw_ref, lambda hs: carry_scr[:, hs], None)
        for hs, (o, carry) in zip(_sb_heads(), res):
            out_scr[:, hs] += o
            carry_scr[:, hs] = carry
        return 0

    lax.fori_loop(0, ntiles, body, 0)
    o_ref[...] = out_scr[...].astype(o_ref.dtype)


def _cumw():
    n = 128
    later = lax.broadcasted_iota(jnp.int32, (n, n), 0) > lax.broadcasted_iota(jnp.int32, (n, n), 1)
    half = jnp.concatenate([later.astype(F32), jnp.ones((n, n), F32)], axis=1)
    return jnp.concatenate([half, half], axis=0).astype(BF16)


def _stick_breaking_prompt(sq, sk, sv, nbatch, seq):
    n = sq.shape[0]
    qb = SB_QBLOCK
    nq = seq // qb
    assert seq % SB_TILE == 0
    tspec = pl.BlockSpec((qb, MIX_W), lambda b, i: (b * nq + i, 0))
    pspec = pl.BlockSpec((1, seq, MIX_W), lambda b, i: (b, 0, 0))
    return pl.pallas_call(
        _sb_prompt_kernel,
        out_shape=jax.ShapeDtypeStruct((n, MIX_W), BF16),
        grid=(nbatch, nq),
        in_specs=[tspec, pspec, pspec, pl.BlockSpec((256, 256), lambda b, i: (0, 0))],
        out_specs=tspec,
        scratch_shapes=[pltpu.VMEM((qb, MIX_W), F32), pltpu.VMEM((qb, MIX_W), F32)],
        compiler_params=_cparams("arbitrary", "arbitrary"),
        name="stick_breaking",
    )(sq, sk.reshape(nbatch, seq, MIX_W), sv.reshape(nbatch, seq, MIX_W), _cumw())


def _stick_breaking_sample(sq, sk, sv, k_cache, v_cache, nbatch, seq):
    n = sq.shape[0]
    past_len = k_cache.shape[2]
    tile = SB_CACHE_TILE if past_len % SB_CACHE_TILE == 0 else SB_TILE
    assert seq <= 128 and past_len % tile == 0
    tspec = pl.BlockSpec((seq, MIX_W), lambda b: (b, 0))
    hbm = pl.BlockSpec(memory_space=pl.ANY)
    return pl.pallas_call(
        _sb_sample_kernel,
        out_shape=jax.ShapeDtypeStruct((n, MIX_W), BF16),
        grid=(nbatch,),
        in_specs=[tspec, tspec, tspec, hbm, hbm, pl.BlockSpec((256, 256), lambda b: (0, 0))],
        out_specs=tspec,
        scratch_shapes=[pltpu.VMEM((128, MIX_W), F32), pltpu.VMEM((128, MIX_W), F32),
                        pltpu.VMEM((seq, MIX_W), F32), pltpu.VMEM((seq, MIX_W), F32),
                        pltpu.VMEM((2, SB_HEADS, tile, HEAD_DIM), F32),
                        pltpu.VMEM((2, SB_HEADS, tile, HEAD_DIM), F32),
                        pltpu.SemaphoreType.DMA((2, 2, SB_HEADS))],
        compiler_params=_cparams("arbitrary"),
        name="stick_breaking_cached",
    )(sq, sk, sv, k_cache, v_cache, _cumw())


def _post_kernel(x_ref, ret_ref, so_ref, mod_ref, wo_ref, g2_ref, wqt_ref, keys_ref,
                 h1_ref, bt_ref, st_ref, b_scr, *, groups, rows):
    attn = _dot(ret_ref[...], wo_ref[0:MIX_W, :]) + _dot(so_ref[...], wo_ref[MIX_W:2 * MIX_W, :])
    for g in range(groups):
        r = slice(g * rows, (g + 1) * rows)
        h1 = x_ref[r, :] + mod_ref[g, 2:3, :] * attn[r, :]
        h1_ref[r, :] = h1
        b_scr[r, :] = _rms(h1) * g2_ref[...] * (1.0 + mod_ref[g, 4:5, :]) + mod_ref[g, 3:4, :]
    bt = b_scr[...].T.astype(BF16)
    bt_ref[...] = bt
    qt = _dot(wqt_ref[...], bt).astype(BF16)
    for hp in range(2 * PEER_HEADS):
        rs = slice(hp * HEAD_DIM, (hp + 1) * HEAD_DIM)
        st_ref[rs, :] = _dot(keys_ref[hp % 2, hp // 2], qt[rs, :])


def _post(x, ret, so, mod, w_out, g2, wq_t, keys, rows_per_batch, tm):
    n, d = x.shape
    nq = wq_t.shape[0]
    groups, mod_spec = _mod_blockspec(tm, rows_per_batch, d, 1)
    return pl.pallas_call(
        functools.partial(_post_kernel, groups=groups, rows=tm // groups),
        out_shape=(jax.ShapeDtypeStruct((n, d), F32),
                   jax.ShapeDtypeStruct((d, n), BF16),
                   jax.ShapeDtypeStruct((nq, n), F32)),
        grid=(n // tm,),
        in_specs=[pl.BlockSpec((tm, d), lambda i: (i, 0)),
                  pl.BlockSpec((tm, MIX_W), lambda i: (i, 0)),
                  pl.BlockSpec((tm, MIX_W), lambda i: (i, 0)),
                  mod_spec,
                  pl.BlockSpec((2 * MIX_W, d), lambda i: (0, 0)),
                  pl.BlockSpec((1, d), lambda i: (0, 0)),
                  pl.BlockSpec((nq, d), lambda i: (0, 0)),
                  pl.BlockSpec(keys.shape, lambda i: (0, 0, 0, 0))],
        out_specs=(pl.BlockSpec((tm, d), lambda i: (i, 0)),
                   pl.BlockSpec((d, tm), lambda i: (0, i)),
                   pl.BlockSpec((nq, tm), lambda i: (0, i))),
        scratch_shapes=[pltpu.VMEM((tm, d), F32)],
        compiler_params=_cparams("arbitrary"),
        name="post_mix",
    )(x, ret, so, mod, w_out, g2, wq_t, keys)


def _stack_rows(rows):
    t = rows[0].shape[1]
    ridx = lax.broadcasted_iota(jnp.int32, (8, t), 0)
    out = jnp.broadcast_to(rows[0], (8, t))
    for r in range(1, len(rows)):
        out = jnp.where(ridx == r, rows[r], out)
    return out


def _top_values(x, count):
    vals = []
    for _ in range(count):
        m = jnp.max(x, axis=0, keepdims=True)
        vals.append(m)
        x = jnp.where(x == m, NEG_INF, x)
    return vals


def _bitonic_pairs(n):
    pairs = []
    k = 2
    while k <= n:
        j = k // 2
        while j >= 1:
            for i in range(n):
                l = i ^ j
                if l > i:
                    pairs.append((i, l, (i & k) == 0))
            j //= 2
        k *= 2
    return pairs


def _top16_of_128(x):
    c = [x[8 * v:8 * (v + 1), :] for v in range(16)]
    for i, j, desc in _bitonic_pairs(16):
        hi, lo = jnp.maximum(c[i], c[j]), jnp.minimum(c[i], c[j])
        c[i], c[j] = (hi, lo) if desc else (lo, hi)
    vals = []
    for r in range(PEER_TOPK):
        m = jnp.max(c[0], axis=0, keepdims=True)
        vals.append(m)
        if r + 1 < PEER_TOPK:
            hit = c[0] == m
            depth = PEER_TOPK - r
            for k in range(depth - 1):
                c[k] = jnp.where(hit, c[k + 1], c[k])
            c[depth - 1] = jnp.where(hit, NEG_INF, c[depth - 1])
    return vals


def _count_prefix(test, rows):
    t1 = test(rows[7])
    t2 = test(jnp.where(t1, rows[11], rows[3]))
    lo3 = jnp.where(t2, rows[5], rows[1])
    hi3 = jnp.where(t2, rows[13], rows[9])
    t3 = test(jnp.where(t1, hi3, lo3))
    c00 = jnp.where(t3, rows[2], rows[0])
    c01 = jnp.where(t3, rows[6], rows[4])
    c10 = jnp.where(t3, rows[10], rows[8])
    c11 = jnp.where(t3, rows[14], rows[12])
    t4 = test(jnp.where(t1, jnp.where(t2, c11, c10), jnp.where(t2, c01, c00)))
    t5 = test(rows[15])
    count = jnp.where(t1, 8.0, 0.0) + jnp.where(t2, 4.0, 0.0) + jnp.where(t3, 2.0, 0.0)
    return count + jnp.where(t4, 1.0, 0.0) + jnp.where(t5, 1.0, 0.0)


def _topk_kernel(s_ref, l1_ref, e1_ref, r2_ref, e2_ref):
    tt = s_ref.shape[1]
    ridx = lax.broadcasted_iota(jnp.int32, (8, tt), 0)
    for h in range(PEER_HEADS):
        base = 2 * h * HEAD_DIM
        s1 = s_ref[base:base + HEAD_DIM, :]
        s2 = s_ref[base + HEAD_DIM:base + 2 * HEAD_DIM, :]
        a = _top16_of_128(s1)
        b = _top16_of_128(s2)
        a_lo, a_hi = _stack_rows(a[0:8]), _stack_rows(a[8:16])
        b_lo, b_hi = _stack_rows(b[0:8]), _stack_rows(b[8:16])
        b_mid = jnp.where(ridx >= 5, b_lo, NEG_INF)
        cand = jnp.concatenate(
            [a_lo + b[l] for l in range(5)] + [a_hi + b[0], b_hi + a[0], b_mid + a[0], b_mid + a[1]], axis=0)
        top = _top_values(cand, PEER_TOPK)
        tau = top[PEER_TOPK - 1]
        z = jnp.zeros_like(tau)
        for v in top:
            z = z + jnp.exp(v - top[0])
        l1_ref[h] = _count_prefix(lambda row: s1 + row >= tau, b)
        e1_ref[h] = jnp.exp(s1 - a[0]) / z
        r2_ref[h * HEAD_DIM:(h + 1) * HEAD_DIM, :] = _count_prefix(lambda row: row > s2, b)
        e2_ref[h * HEAD_DIM:(h + 1) * HEAD_DIM, :] = jnp.exp(s2 - b[0])


def _topk_stats(st, tt):
    nq, n = st.shape
    rows = PEER_HEADS * HEAD_DIM
    spec3 = pl.BlockSpec((PEER_HEADS, HEAD_DIM, tt), lambda i: (0, 0, i))
    spec2 = pl.BlockSpec((rows, tt), lambda i: (0, i))
    return pl.pallas_call(
        _topk_kernel,
        out_shape=(jax.ShapeDtypeStruct((PEER_HEADS, HEAD_DIM, n), F32),
                   jax.ShapeDtypeStruct((PEER_HEADS, HEAD_DIM, n), F32),
                   jax.ShapeDtypeStruct((rows, n), F32),
                   jax.ShapeDtypeStruct((rows, n), F32)),
        grid=(n // tt,),
        in_specs=[pl.BlockSpec((nq, tt), lambda i: (0, i))],
        out_specs=(spec3, spec3, spec2, spec2),
        compiler_params=_cparams("arbitrary"),
        name="peer_topk",
    )(st)


PEER_SUB = 512


def _gelu(x):
    half = 0.5 * x
    inner = x * (0.7978845608028654 + (0.7978845608028654 * 0.044715) * (x * x))
    return half + half * jnp.tanh(inner)


def _peer_kernel(bt_ref, u_ref, vt_ref, r2_ref, e2_ref, l1_ref, e1_ref, h1_ref, mod_ref, gf_ref, y_ref,
                 acc_ref, wa_scr, act_scr, r2_scr, e2_scr, *, groups, rows, ec):
    k = pl.program_id(1)
    t = bt_ref.shape[1]

    @pl.when(k == 0)
    def _():
        acc_ref[...] = jnp.zeros_like(acc_ref)
        r2_scr[...] = r2_ref[...].astype(BF16)
        e2_scr[...] = e2_ref[...].astype(BF16)

    def bcast_bf16(row):
        return jnp.tile(jnp.broadcast_to(row, (16, 128)).astype(BF16), (HEAD_DIM // 16, 1))

    nsub = PEER_SUB // HEAD_DIM
    nchunk = ec // PEER_SUB

    def activations(sc):
        es = slice(sc * PEER_SUB, (sc + 1) * PEER_SUB)
        act_scr[es, :] = _gelu(_dot(u_ref[es, :], bt_ref[...])).astype(BF16)

    def gates(sc):
        for tl in range(t // 128):
            ls = slice(tl * 128, (tl + 1) * 128)
            w = [jnp.zeros((HEAD_DIM, 128), BF16) for _ in range(nsub)]
            for h in range(PEER_HEADS):
                hs = slice(h * HEAD_DIM, (h + 1) * HEAD_DIM)
                rank2 = r2_scr[hs, ls]
                e2 = e2_scr[hs, ls]
                for ii in range(nsub):
                    il = sc * nsub + ii
                    limit1 = bcast_bf16(l1_ref[h, il:il + 1, ls])
                    e1 = bcast_bf16(e1_ref[h, il:il + 1, ls])
                    w[ii] = w[ii] + jnp.where(rank2 < limit1, e2 * e1, 0)
            for ii in range(nsub):
                rs = slice(sc * PEER_SUB + ii * HEAD_DIM, sc * PEER_SUB + (ii + 1) * HEAD_DIM)
                wa_scr[rs, ls] = w[ii] * act_scr[rs, ls]

    def values(sc):
        es = slice(sc * PEER_SUB, (sc + 1) * PEER_SUB)
        acc_ref[...] += _dot(vt_ref[:, es], wa_scr[es, :])

    activations(0)
    for sc in range(nchunk):
        if sc + 1 < nchunk:
            activations(sc + 1)
        gates(sc)
        values(sc)

    @pl.when(k == pl.num_programs(1) - 1)
    def _():
        out = acc_ref[...].T
        for g in range(groups):
            r = slice(g * rows, (g + 1) * rows)
            h2 = h1_ref[r, :] + mod_ref[g, 5:6, :] * out[r, :]
            y_ref[r, :] = _rms(h2) * gf_ref[...]


def _peer(bt, u, vt, l1, e1, r2, e2, h1, mod, gf, rows_per_batch, t, ec):
    d, n = bt.shape
    ne = u.shape[0]
    nr = r2.shape[0]
    groups, mod_spec = _mod_blockspec(t, rows_per_batch, d, 2)
    spec_i = pl.BlockSpec((PEER_HEADS, ec // HEAD_DIM, t), lambda i, k: (0, k, i))
    spec_j = pl.BlockSpec((nr, t), lambda i, k: (0, i))
    return pl.pallas_call(
        functools.partial(_peer_kernel, groups=groups, rows=t // groups, ec=ec),
        out_shape=jax.ShapeDtypeStruct((n, d), F32),
        grid=(n // t, ne // ec),
        in_specs=[pl.BlockSpec((d, t), lambda i, k: (0, i)),
                  pl.BlockSpec((ec, d), lambda i, k: (k, 0)),
                  pl.BlockSpec((d, ec), lambda i, k: (0, k)),
                  spec_j, spec_j, spec_i, spec_i,
                  pl.BlockSpec((t, d), lambda i, k: (i, 0)),
                  mod_spec,
                  pl.BlockSpec((1, d), lambda i, k: (0, 0))],
        out_specs=pl.BlockSpec((t, d), lambda i, k: (i, 0)),
        scratch_shapes=[pltpu.VMEM((d, t), F32), pltpu.VMEM((ec, t), BF16), pltpu.VMEM((ec, t), BF16),
                        pltpu.VMEM((nr, t), BF16), pltpu.VMEM((nr, t), BF16)],
        compiler_params=_cparams("arbitrary", "arbitrary"),
        name="peer_dense",
    )(bt, u, vt, r2, e2, l1, e1, h1, mod, gf)


def _rope_tables(pos):
    half = HEAD_DIM // 2
    inv_freq = ROPE_BASE ** (-jnp.arange(half, dtype=F32) / half)
    ang = pos.astype(F32)[:, None] * inv_freq[None, :]
    cos, sin = jnp.cos(ang), jnp.sin(ang)
    return jnp.concatenate([cos, cos], axis=1), jnp.concatenate([-sin, sin], axis=1)


def _tile(n, pref):
    t = min(n, pref)
    assert n % t == 0
    return t


def _run_group(x, mod, pos, s0, k_past, v_past, params):
    (g1, w_in, ret_g, w_out, g2, wq_t, keys, u, vt, gf) = params
    nbatch, seq, d = x.shape
    n = nbatch * seq
    x2 = x.reshape(n, d)

    tm = _tile(n, 512)
    reps = max(1, tm // seq)
    cos_t, sin_t = _rope_tables(pos)
    cos_t, sin_t = jnp.tile(cos_t, (reps, 1)), jnp.tile(sin_t, (reps, 1))
    rq, rk, rv, rg, sq, sk, sv = _inproj(x2, mod, g1, w_in, cos_t, sin_t, seq, tm)

    chunk = min(seq, RET_CHUNK)
    ret, s_new = _retention(rq, rk, rv, rg, s0, ret_g, nbatch, seq, chunk)

    if k_past is None:
        so = _stick_breaking_prompt(sq, sk, sv, nbatch, seq)
    else:
        so = _stick_breaking_sample(sq, sk, sv, k_past, v_past, nbatch, seq)

    tp = _tile(n, 512)
    h1, bt, st = _post(x2, ret, so, mod, w_out, g2, wq_t, keys, seq, tp)
    l1, e1, r2, e2 = _topk_stats(st, _tile(n, 256))
    y = _peer(bt, u, vt, l1, e1, r2, e2, h1, mod, gf, seq, tp, 2048)
    return (y.reshape(nbatch, seq, d),
            s_new.reshape(1, nbatch, RET_HEADS, HEAD_DIM, HEAD_DIM),
            sk.reshape(1, nbatch, seq, SB_HEADS, HEAD_DIM),
            sv.reshape(1, nbatch, seq, SB_HEADS, HEAD_DIM))


def kernel(x_prompt, x_sample, c_prompt, c_sample, state_ret, cache_sb_k, cache_sb_v, w_ada, b_ada, norm1_g,
           w_in, ret_norm_g, w_out, norm2_g, peer_w_query, peer_sub_keys, peer_u, peer_v, final_norm_g):
    assert w_ada.shape[0] == 1, "single-layer step"
    bp, seq_p, d = x_prompt.shape
    bs, seq_s, _ = x_sample.shape
    past_len = cache_sb_k.shape[2]

    mod = _adaln(jnp.concatenate([c_prompt, c_sample], axis=0), w_ada[0], b_ada[0]).reshape(bp + bs, 6, d)
    params = (norm1_g[0].reshape(1, d), w_in[0].astype(BF16), ret_norm_g[0].reshape(1, MIX_W),
              w_out[0].astype(BF16), norm2_g[0].reshape(1, d), peer_w_query[0].T.astype(BF16),
              peer_sub_keys[0].astype(BF16), peer_u[0].astype(BF16), peer_v[0].T.astype(BF16),
              final_norm_g.reshape(1, d))

    zeros_state = jnp.zeros((bp, RET_HEADS, HEAD_DIM, HEAD_DIM), F32)
    y_p, s_p, k_p, v_p = _run_group(x_prompt, mod[:bp], jnp.arange(seq_p), zeros_state, None, None, params)
    y_s, s_s, k_s, v_s = _run_group(x_sample, mod[bp:], past_len + jnp.arange(seq_s), state_ret[0],
                                    cache_sb_k, cache_sb_v, params)
    return (y_p, y_s, s_p, k_p, v_p, s_s, k_s, v_s)
```

```python
import functools

import jax
import jax.numpy as jnp
from jax import lax
from jax.experimental import pallas as pl
from jax.experimental.pallas import tpu as pltpu

F32 = jnp.float32
BF16 = jnp.bfloat16

NORM_EPS = 1e-6
ROPE_BASE = 10000.0
HEAD_DIM = 128
RET_HEADS = 4
SB_HEADS = 4
MIX_W = RET_HEADS * HEAD_DIM
PEER_HEADS = 8
PEER_TOPK = 16
SB_QBLOCK = 512
RET_CHUNK = 512
VMEM_LIMIT = 56 * 1024 * 1024
NEG_INF = float("-inf")


def _cparams(*sem):
    return pltpu.CompilerParams(dimension_semantics=sem, vmem_limit_bytes=VMEM_LIMIT)


def _dot(a, b):
    return jnp.dot(a, b, preferred_element_type=F32)


def _dot_nt(a, b):
    return lax.dot_general(a, b, (((1,), (1,)), ((), ())), preferred_element_type=F32)


def _split_bf16(x):
    hi = x.astype(BF16)
    lo = (x - hi.astype(F32)).astype(BF16)
    return hi, lo


def _rms(x):
    return x * lax.rsqrt(jnp.mean(x * x, axis=-1, keepdims=True) + NORM_EPS)


def _silu(x):
    return x / (1.0 + jnp.exp(-x))


def _adaln_kernel(c_ref, w_ref, b_ref, o_ref):
    s = _silu(c_ref[...])
    s_hi, s_lo = _split_bf16(s)
    w_hi, w_lo = _split_bf16(w_ref[...])
    o_ref[...] = _dot(s_hi, w_hi) + _dot(s_hi, w_lo) + _dot(s_lo, w_hi) + b_ref[...]


def _adaln(c, w_ada, b_ada):
    nb, d = c.shape
    n = w_ada.shape[1]
    tn = 1024
    return pl.pallas_call(
        _adaln_kernel,
        out_shape=jax.ShapeDtypeStruct((nb, n), F32),
        grid=(n // tn,),
        in_specs=[pl.BlockSpec((nb, d), lambda j: (0, 0)),
                  pl.BlockSpec((d, tn), lambda j: (0, j)),
                  pl.BlockSpec((1, tn), lambda j: (0, j))],
        out_specs=pl.BlockSpec((nb, tn), lambda j: (0, j)),
        compiler_params=_cparams("arbitrary"),
        name="adaln",
    )(c, w_ada, b_ada.reshape(1, n))


def _mod_blockspec(tm, rows_per_batch, d, ngrid):
    groups = max(1, tm // rows_per_batch)
    if groups == 1:
        per = rows_per_batch // tm
        if ngrid == 1:
            return groups, pl.BlockSpec((1, 6, d), lambda i: (i // per, 0, 0))
        return groups, pl.BlockSpec((1, 6, d), lambda i, k: (i // per, 0, 0))
    if ngrid == 1:
        return groups, pl.BlockSpec((groups, 6, d), lambda i: (i, 0, 0))
    return groups, pl.BlockSpec((groups, 6, d), lambda i, k: (i, 0, 0))


def _inproj_kernel(x_ref, mod_ref, g1_ref, w_ref, cos_ref, sin_ref,
                   rq_ref, rk_ref, rv_ref, rg_ref, sq_ref, sk_ref, sv_ref, a_scr, *, groups, rows):
    for g in range(groups):
        r = slice(g * rows, (g + 1) * rows)
        xn = _rms(x_ref[r, :]) * g1_ref[...]
        a = xn * (1.0 + mod_ref[g, 1:2, :]) + mod_ref[g, 0:1, :]
        a_scr[r, :] = a.astype(BF16)
    y = _dot(a_scr[...], w_ref[...])
    cos = cos_ref[...]
    sin = sin_ref[...]
    for h in range(RET_HEADS):
        c = slice(h * HEAD_DIM, (h + 1) * HEAD_DIM)
        yq = y[:, h * HEAD_DIM:(h + 1) * HEAD_DIM]
        yk = y[:, MIX_W + h * HEAD_DIM:MIX_W + (h + 1) * HEAD_DIM]
        rq = (yq * cos + pltpu.roll(yq, HEAD_DIM // 2, 1) * sin) * (HEAD_DIM ** -0.5)
        rk = yk * cos + pltpu.roll(yk, HEAD_DIM // 2, 1) * sin
        rq_ref[:, c] = rq.astype(rq_ref.dtype)
        rk_ref[:, c] = rk.astype(rk_ref.dtype)
    rv_ref[...] = y[:, 2 * MIX_W:3 * MIX_W].astype(rv_ref.dtype)
    rg_ref[...] = y[:, 3 * MIX_W:4 * MIX_W].astype(rg_ref.dtype)
    sq_ref[...] = y[:, 4 * MIX_W:5 * MIX_W].astype(sq_ref.dtype)
    sk_ref[...] = y[:, 5 * MIX_W:6 * MIX_W]
    sv_ref[...] = y[:, 6 * MIX_W:7 * MIX_W]


def _inproj(x, mod, g1, w_in, cos_t, sin_t, rows_per_batch, tm):
    n, d = x.shape
    ncol = w_in.shape[1]
    groups, mod_spec = _mod_blockspec(tm, rows_per_batch, d, 1)
    npos = cos_t.shape[0] // tm
    tok = lambda dt: jax.ShapeDtypeStruct((n, MIX_W), dt)
    tspec = pl.BlockSpec((tm, MIX_W), lambda i: (i, 0))
    return pl.pallas_call(
        functools.partial(_inproj_kernel, groups=groups, rows=tm // groups),
        out_shape=(tok(BF16), tok(BF16), tok(BF16), tok(BF16), tok(BF16), tok(F32), tok(F32)),
        grid=(n // tm,),
        in_specs=[pl.BlockSpec((tm, d), lambda i: (i, 0)),
                  mod_spec,
                  pl.BlockSpec((1, d), lambda i: (0, 0)),
                  pl.BlockSpec((d, ncol), lambda i: (0, 0)),
                  pl.BlockSpec((tm, HEAD_DIM), lambda i: (i % npos, 0)),
                  pl.BlockSpec((tm, HEAD_DIM), lambda i: (i % npos, 0))],
        out_specs=(tspec,) * 7,
        scratch_shapes=[pltpu.VMEM((tm, d), BF16)],
        compiler_params=_cparams("arbitrary"),
        name="inproj",
    )(x, mod, g1, w_in, cos_t, sin_t)


def _retention_kernel(q_ref, k_ref, v_ref, g_ref, s0_ref, intra_ref, qd_ref, kd_ref, cd_ref, gn_ref,
                      o_ref, sout_ref, s_scr):
    c = pl.program_id(1)

    @pl.when(c == 0)
    def _():
        s_scr[...] = s0_ref[0]

    for h in range(RET_HEADS):
        cs = slice(h * HEAD_DIM, (h + 1) * HEAD_DIM)
        q = q_ref[:, cs]
        k = k_ref[:, cs]
        v = v_ref[:, cs]
        state = s_scr[h]
        scores = _dot_nt(q, k) * intra_ref[h]
        o = _dot(scores.astype(BF16), v) + _dot(q, state.astype(BF16)) * qd_ref[h]
        kdt = (k.astype(F32) * kd_ref[h]).T.astype(BF16)
        s_scr[h] = cd_ref[h] * state + _dot(kdt, v)
        cen = o - jnp.mean(o, axis=-1, keepdims=True)
        on = cen * lax.rsqrt(jnp.mean(cen * cen, axis=-1, keepdims=True) + NORM_EPS)
        gate = _silu(g_ref[:, cs].astype(F32))
        o_ref[:, cs] = (on * gn_ref[:, cs] * gate).astype(o_ref.dtype)

    @pl.when(c == pl.num_programs(1) - 1)
    def _():
        sout_ref[0] = s_scr[...]


def _retention_tables(chunk):
    log_g = jnp.log1p(-jnp.exp2(-5.0 - jnp.arange(RET_HEADS, dtype=F32)))
    idx = jnp.arange(chunk, dtype=F32)
    diff = idx[:, None] - idx[None, :]
    causal = diff >= 0
    intra = jnp.where(causal[None], jnp.exp(jnp.where(causal, diff, 0.0)[None] * log_g[:, None, None]), 0.0)
    qd = jnp.exp((idx[None, :] + 1.0) * log_g[:, None])
    kd = jnp.exp((chunk - 1.0 - idx)[None, :] * log_g[:, None])
    cd = jnp.exp(chunk * log_g)
    bc = lambda t: jnp.broadcast_to(t[:, :, None], (RET_HEADS, chunk, HEAD_DIM))
    return intra, bc(qd), bc(kd), jnp.broadcast_to(cd[:, None, None], (RET_HEADS, 1, HEAD_DIM))


def _retention(rq, rk, rv, rg, s0, ret_norm_g, nbatch, seq, chunk):
    n = rq.shape[0]
    nc = seq // chunk
    intra, qd, kd, cd = _retention_tables(chunk)
    tspec = pl.BlockSpec((chunk, MIX_W), lambda b, c: (b * nc + c, 0))
    full = lambda shape: pl.BlockSpec(shape, lambda b, c: (0,) * len(shape))
    sspec = pl.BlockSpec((1, RET_HEADS, HEAD_DIM, HEAD_DIM), lambda b, c: (b, 0, 0, 0))
    return pl.pallas_call(
        _retention_kernel,
        out_shape=(jax.ShapeDtypeStruct((n, MIX_W), BF16),
                   jax.ShapeDtypeStruct((nbatch, RET_HEADS, HEAD_DIM, HEAD_DIM), F32)),
        grid=(nbatch, nc),
        in_specs=[tspec, tspec, tspec, tspec, sspec,
                  full((RET_HEADS, chunk, chunk)), full((RET_HEADS, chunk, HEAD_DIM)),
                  full((RET_HEADS, chunk, HEAD_DIM)), full((RET_HEADS, 1, HEAD_DIM)), full((1, MIX_W))],
        out_specs=(tspec, sspec),
        scratch_shapes=[pltpu.VMEM((RET_HEADS, HEAD_DIM, HEAD_DIM), F32)],
        compiler_params=_cparams("arbitrary", "arbitrary"),
        name="retention",
    )(rq, rk, rv, rg, s0, intra, qd, kd, cd, ret_norm_g)


def _log_sigmoid(z):
    return jnp.minimum(z, 0.0) - jnp.log(1.0 + jnp.exp(-jnp.abs(z)))


SB_TILE = 512
SB_CACHE_TILE = 2048


def _sb_scores(q, kblk):
    return _dot_nt(q, kblk) * (HEAD_DIM ** -0.5)


def _sb_cumulate(z, cumw, mask):
    nsub = z.shape[1] // 128
    log_beta = _log_sigmoid(z)
    log_keep = log_beta - z
    if mask is not None:
        log_keep = jnp.where(mask, log_keep, 0.0)
    parts = _split_bf16(log_keep)
    lhs = jnp.concatenate(
        [jnp.concatenate([p[:, c * 128:(c + 1) * 128] for p in parts], axis=1) for c in range(nsub)], axis=0)
    return log_beta, _dot(lhs, cumw)


def _sb_weights(log_beta, cw, vblk, carry, mask):
    nsub = vblk.shape[0] // 128
    qb = cw.shape[0] // nsub
    a = [None] * nsub
    for c in reversed(range(nsub)):
        a[c] = jnp.exp(log_beta[:, c * 128:(c + 1) * 128] + (cw[c * qb:(c + 1) * qb, 0:128] + carry))
        carry = carry + cw[c * qb:(c + 1) * qb, 128:256]
    a = jnp.concatenate(a, axis=1)
    if mask is not None:
        a = jnp.where(mask, a, 0.0)
    return _dot(a.astype(BF16), vblk), carry


def _sb_tiles(q_ref, k_of, v_of, cumw_ref, carry_of, mask):
    heads = _sb_heads()
    z, cw, res = {}, {}, {}
    for step in range(len(heads) + 2):
        if step < len(heads):
            z[step] = _sb_scores(q_ref[:, heads[step]], k_of(heads[step]))
        if 0 <= step - 1 < len(heads):
            cw[step - 1] = _sb_cumulate(z.pop(step - 1), cumw_ref[...], mask)
        if 0 <= step - 2 < len(heads):
            hs = heads[step - 2]
            res[step - 2] = _sb_weights(*cw.pop(step - 2), v_of(hs), carry_of(hs), mask)
    return [res[h] for h in range(len(heads))]


def _sb_heads():
    return [slice(h * HEAD_DIM, (h + 1) * HEAD_DIM) for h in range(SB_HEADS)]


def _sb_past_tiles(q_ref, kp_ref, vp_ref, cumw_ref, out_scr, carry_scr, ntiles):
    def body(it, _):
        start = pl.multiple_of((ntiles - 1 - it) * SB_TILE, SB_TILE)
        res = _sb_tiles(q_ref, lambda hs: kp_ref[0, pl.ds(start, SB_TILE), hs].astype(BF16),
                        lambda hs: vp_ref[0, pl.ds(start, SB_TILE), hs].astype(BF16),
                        cumw_ref, lambda hs: carry_scr[:, hs], None)
        for hs, (o, carry) in zip(_sb_heads(), res):
            out_scr[:, hs] += o
            carry_scr[:, hs] = carry
        return 0

    lax.fori_loop(0, ntiles, body, 0)


def _sb_prompt_kernel(q_ref, kp_ref, vp_ref, cumw_ref, o_ref, out_scr, carry_scr):
    i = pl.program_id(1)
    qb = q_ref.shape[0]
    last = (i * qb) // SB_TILE
    start = pl.multiple_of(last * SB_TILE, SB_TILE)
    q_pos = i * qb + lax.broadcasted_iota(jnp.int32, (qb, SB_TILE), 0)
    k_pos = start + lax.broadcasted_iota(jnp.int32, (qb, SB_TILE), 1)
    res = _sb_tiles(q_ref, lambda hs: kp_ref[0, pl.ds(start, SB_TILE), hs].astype(BF16),
                    lambda hs: vp_ref[0, pl.ds(start, SB_TILE), hs].astype(BF16),
                    cumw_ref, lambda hs: jnp.zeros((qb, 128), F32), k_pos < q_pos)
    for hs, (o, carry) in zip(_sb_heads(), res):
        out_scr[:, hs] = o
        carry_scr[:, hs] = carry
    _sb_past_tiles(q_ref, kp_ref, vp_ref, cumw_ref, out_scr, carry_scr, last)
    o_ref[...] = out_scr[...].astype(o_ref.dtype)


def _sb_sample_kernel(q_ref, kn_ref, vn_ref, kc_hbm, vc_hbm, cumw_ref, o_ref,
                      kpad, vpad, out_scr, carry_scr, kbuf, vbuf, sem):
    b = pl.program_id(0)
    qb = q_ref.shape[0]
    tile = kbuf.shape[2]
    ntiles = kc_hbm.shape[2] // tile
    head_of = {hs.start: h for h, hs in enumerate(_sb_heads())}

    def tile_copies(it, slot):
        start = pl.multiple_of((ntiles - 1 - it) * tile, tile)
        cps = []
        for h in range(SB_HEADS):
            cps.append(pltpu.make_async_copy(kc_hbm.at[0, b, pl.ds(start, tile), h, :],
                                             kbuf.at[slot, h], sem.at[0, slot, h]))
            cps.append(pltpu.make_async_copy(vc_hbm.at[0, b, pl.ds(start, tile), h, :],
                                             vbuf.at[slot, h], sem.at[1, slot, h]))
        return cps

    for cp in tile_copies(0, 0):
        cp.start()

    kpad[...] = jnp.zeros_like(kpad)
    vpad[...] = jnp.zeros_like(vpad)
    kpad[0:qb, :] = kn_ref[...]
    vpad[0:qb, :] = vn_ref[...]
    tq = lax.broadcasted_iota(jnp.int32, (qb, 128), 0)
    ts = lax.broadcasted_iota(jnp.int32, (qb, 128), 1)
    res = _sb_tiles(q_ref, lambda hs: kpad[:, hs].astype(BF16), lambda hs: vpad[:, hs].astype(BF16),
                    cumw_ref, lambda hs: jnp.zeros((qb, 128), F32), ts < tq)
    for hs, (o, carry) in zip(_sb_heads(), res):
        out_scr[:, hs] = o
        carry_scr[:, hs] = carry

    def body(it, _):
        slot = it % 2

        @pl.when(it + 1 < ntiles)
        def _():
            for cp in tile_copies(it + 1, 1 - slot):
                cp.start()

        for cp in tile_copies(it, slot):
            cp.wait()
        res = _sb_tiles(q_ref, lambda hs: kbuf[slot, head_of[hs.start]].astype(BF16),
                        lambda hs: vbuf[slot, head_of[hs.start]].astype(BF16),
                        cumw_ref, lambda hs: carry_scr[:, hs], None)
        for hs, (o, carry) in zip(_sb_heads(), res):
            out_scr[:, hs] += o
            carry_scr[:, hs] = carry
        return 0

    lax.fori_loop(0, ntiles, body, 0)
    o_ref[...] = out_scr[...].astype(o_ref.dtype)


def _cumw():
    n = 128
    later = lax.broadcasted_iota(jnp.int32, (n, n), 0) > lax.broadcasted_iota(jnp.int32, (n, n), 1)
    half = jnp.concatenate([later.astype(F32), jnp.ones((n, n), F32)], axis=1)
    return jnp.concatenate([half, half], axis=0).astype(BF16)


def _stick_breaking_prompt(sq, sk, sv, nbatch, seq):
    n = sq.shape[0]
    qb = SB_QBLOCK
    nq = seq // qb
    assert seq % SB_TILE == 0
    tspec = pl.BlockSpec((qb, MIX_W), lambda b, i: (b * nq + i, 0))
    pspec = pl.BlockSpec((1, seq, MIX_W), lambda b, i: (b, 0, 0))
    return pl.pallas_call(
        _sb_prompt_kernel,
        out_shape=jax.ShapeDtypeStruct((n, MIX_W), BF16),
        grid=(nbatch, nq),
        in_specs=[tspec, pspec, pspec, pl.BlockSpec((256, 256), lambda b, i: (0, 0))],
        out_specs=tspec,
        scratch_shapes=[pltpu.VMEM((qb, MIX_W), F32), pltpu.VMEM((qb, MIX_W), F32)],
        compiler_params=_cparams("arbitrary", "arbitrary"),
        name="stick_breaking",
    )(sq, sk.reshape(nbatch, seq, MIX_W), sv.reshape(nbatch, seq, MIX_W), _cumw())


def _stick_breaking_sample(sq, sk, sv, k_cache, v_cache, nbatch, seq):
    n = sq.shape[0]
    past_len = k_cache.shape[2]
    tile = SB_CACHE_TILE if past_len % SB_CACHE_TILE == 0 else SB_TILE
    assert seq <= 128 and past_len % tile == 0
    tspec = pl.BlockSpec((seq, MIX_W), lambda b: (b, 0))
    hbm = pl.BlockSpec(memory_space=pl.ANY)
    return pl.pallas_call(
        _sb_sample_kernel,
        out_shape=jax.ShapeDtypeStruct((n, MIX_W), BF16),
        grid=(nbatch,),
        in_specs=[tspec, tspec, tspec, hbm, hbm, pl.BlockSpec((256, 256), lambda b: (0, 0))],
        out_specs=tspec,
        scratch_shapes=[pltpu.VMEM((128, MIX_W), F32), pltpu.VMEM((128, MIX_W), F32),
                        pltpu.VMEM((seq, MIX_W), F32), pltpu.VMEM((seq, MIX_W), F32),
                        pltpu.VMEM((2, SB_HEADS, tile, HEAD_DIM), F32),
                        pltpu.VMEM((2, SB_HEADS, tile, HEAD_DIM), F32),
                        pltpu.SemaphoreType.DMA((2, 2, SB_HEADS))],
        compiler_params=_cparams("arbitrary"),
        name="stick_breaking_cached",
    )(sq, sk, sv, k_cache, v_cache, _cumw())


def _post_kernel(x_ref, ret_ref, so_ref, mod_ref, wo_ref, g2_ref, wqt_ref, keys_ref,
                 h1_ref, bt_ref, st_ref, b_scr, *, groups, rows):
    attn = _dot(ret_ref[...], wo_ref[0:MIX_W, :]) + _dot(so_ref[...], wo_ref[MIX_W:2 * MIX_W, :])
    for g in range(groups):
        r = slice(g * rows, (g + 1) * rows)
        h1 = x_ref[r, :] + mod_ref[g, 2:3, :] * attn[r, :]
        h1_ref[r, :] = h1
        b_scr[r, :] = _rms(h1) * g2_ref[...] * (1.0 + mod_ref[g, 4:5, :]) + mod_ref[g, 3:4, :]
    bt = b_scr[...].T.astype(BF16)
    bt_ref[...] = bt
    qt = _dot(wqt_ref[...], bt).astype(BF16)
    for hp in range(2 * PEER_HEADS):
        rs = slice(hp * HEAD_DIM, (hp + 1) * HEAD_DIM)
        st_ref[rs, :] = _dot(keys_ref[hp % 2, hp // 2], qt[rs, :])


def _post(x, ret, so, mod, w_out, g2, wq_t, keys, rows_per_batch, tm):
    n, d = x.shape
    nq = wq_t.shape[0]
    groups, mod_spec = _mod_blockspec(tm, rows_per_batch, d, 1)
    return pl.pallas_call(
        functools.partial(_post_kernel, groups=groups, rows=tm // groups),
        out_shape=(jax.ShapeDtypeStruct((n, d), F32),
                   jax.ShapeDtypeStruct((d, n), BF16),
                   jax.ShapeDtypeStruct((nq, n), F32)),
        grid=(n // tm,),
        in_specs=[pl.BlockSpec((tm, d), lambda i: (i, 0)),
                  pl.BlockSpec((tm, MIX_W), lambda i: (i, 0)),
                  pl.BlockSpec((tm, MIX_W), lambda i: (i, 0)),
                  mod_spec,
                  pl.BlockSpec((2 * MIX_W, d), lambda i: (0, 0)),
                  pl.BlockSpec((1, d), lambda i: (0, 0)),
                  pl.BlockSpec((nq, d), lambda i: (0, 0)),
                  pl.BlockSpec(keys.shape, lambda i: (0, 0, 0, 0))],
        out_specs=(pl.BlockSpec((tm, d), lambda i: (i, 0)),
                   pl.BlockSpec((d, tm), lambda i: (0, i)),
                   pl.BlockSpec((nq, tm), lambda i: (0, i))),
        scratch_shapes=[pltpu.VMEM((tm, d), F32)],
        compiler_params=_cparams("arbitrary"),
        name="post_mix",
    )(x, ret, so, mod, w_out, g2, wq_t, keys)


def _stack_rows(rows):
    t = rows[0].shape[1]
    ridx = lax.broadcasted_iota(jnp.int32, (8, t), 0)
    out = jnp.broadcast_to(rows[0], (8, t))
    for r in range(1, len(rows)):
        out = jnp.where(ridx == r, rows[r], out)
    return out


def _top_values(x, count):
    vals = []
    for _ in range(count):
        m = jnp.max(x, axis=0, keepdims=True)
        vals.append(m)
        x = jnp.where(x == m, NEG_INF, x)
    return vals


def _bitonic_pairs(n):
    pairs = []
    k = 2
    while k <= n:
        j = k // 2
        while j >= 1:
            for i in range(n):
                l = i ^ j
                if l > i:
                    pairs.append((i, l, (i & k) == 0))
            j //= 2
        k *= 2
    return pairs


def _top16_of_128(x):
    c = [x[8 * v:8 * (v + 1), :] for v in range(16)]
    for i, j, desc in _bitonic_pairs(16):
        hi, lo = jnp.maximum(c[i], c[j]), jnp.minimum(c[i], c[j])
        c[i], c[j] = (hi, lo) if desc else (lo, hi)
    vals = []
    for r in range(PEER_TOPK):
        m = jnp.max(c[0], axis=0, keepdims=True)
        vals.append(m)
        if r + 1 < PEER_TOPK:
            hit = c[0] == m
            depth = PEER_TOPK - r
            for k in range(depth - 1):
                c[k] = jnp.where(hit, c[k + 1], c[k])
            c[depth - 1] = jnp.where(hit, NEG_INF, c[depth - 1])
    return vals


def _count_prefix(test, rows):
    t1 = test(rows[7])
    t2 = test(jnp.where(t1, rows[11], rows[3]))
    lo3 = jnp.where(t2, rows[5], rows[1])
    hi3 = jnp.where(t2, rows[13], rows[9])
    t3 = test(jnp.where(t1, hi3, lo3))
    c00 = jnp.where(t3, rows[2], rows[0])
    c01 = jnp.where(t3, rows[6], rows[4])
    c10 = jnp.where(t3, rows[10], rows[8])
    c11 = jnp.where(t3, rows[14], rows[12])
    t4 = test(jnp.where(t1, jnp.where(t2, c11, c10), jnp.where(t2, c01, c00)))
    t5 = test(rows[15])
    count = jnp.where(t1, 8.0, 0.0) + jnp.where(t2, 4.0, 0.0) + jnp.where(t3, 2.0, 0.0)
    return count + jnp.where(t4, 1.0, 0.0) + jnp.where(t5, 1.0, 0.0)


def _topk_kernel(s_ref, l1_ref, e1_ref, r2_ref, e2_ref):
    tt = s_ref.shape[1]
    ridx = lax.broadcasted_iota(jnp.int32, (8, tt), 0)
    for h in range(PEER_HEADS):
        base = 2 * h * HEAD_DIM
        s1 = s_ref[base:base + HEAD_DIM, :]
        s2 = s_ref[base + HEAD_DIM:base + 2 * HEAD_DIM, :]
        a = _top16_of_128(s1)
        b = _top16_of_128(s2)
        a_lo, a_hi = _stack_rows(a[0:8]), _stack_rows(a[8:16])
        b_lo, b_hi = _stack_rows(b[0:8]), _stack_rows(b[8:16])
        b_mid = jnp.where(ridx >= 5, b_lo, NEG_INF)
        cand = jnp.concatenate(
            [a_lo + b[l] for l in range(5)] + [a_hi + b[0], b_hi + a[0], b_mid + a[0], b_mid + a[1]], axis=0)
        top = _top_values(cand, PEER_TOPK)
        tau = top[PEER_TOPK - 1]
        z = jnp.zeros_like(tau)
        for v in top:
            z = z + jnp.exp(v - top[0])
        l1_ref[h] = _count_prefix(lambda row: s1 + row >= tau, b)
        e1_ref[h] = jnp.exp(s1 - a[0]) / z
        r2_ref[h * HEAD_DIM:(h + 1) * HEAD_DIM, :] = _count_prefix(lambda row: row > s2, b)
        e2_ref[h * HEAD_DIM:(h + 1) * HEAD_DIM, :] = jnp.exp(s2 - b[0])


def _topk_stats(st, tt):
    nq, n = st.shape
    rows = PEER_HEADS * HEAD_DIM
    spec3 = pl.BlockSpec((PEER_HEADS, HEAD_DIM, tt), lambda i: (0, 0, i))
    spec2 = pl.BlockSpec((rows, tt), lambda i: (0, i))
    return pl.pallas_call(
        _topk_kernel,
        out_shape=(jax.ShapeDtypeStruct((PEER_HEADS, HEAD_DIM, n), F32),
                   jax.ShapeDtypeStruct((PEER_HEADS, HEAD_DIM, n), F32),
                   jax.ShapeDtypeStruct((rows, n), F32),
                   jax.ShapeDtypeStruct((rows, n), F32)),
        grid=(n // tt,),
        in_specs=[pl.BlockSpec((nq, tt), lambda i: (0, i))],
        out_specs=(spec3, spec3, spec2, spec2),
        compiler_params=_cparams("arbitrary"),
        name="peer_topk",
    )(st)


PEER_SUB = 1024


def _gelu(x):
    half = 0.5 * x
    inner = x * (0.7978845608028654 + (0.7978845608028654 * 0.044715) * (x * x))
    return half + half * jnp.tanh(inner)


def _peer_kernel(bt_ref, u_ref, vt_ref, r2_ref, e2_ref, l1_ref, e1_ref, h1_ref, mod_ref, gf_ref, y_ref,
                 acc_ref, wa_scr, act_scr, r2_scr, e2_scr, *, groups, rows, ec):
    k = pl.program_id(1)
    t = bt_ref.shape[1]

    @pl.when(k == 0)
    def _():
        acc_ref[...] = jnp.zeros_like(acc_ref)
        r2_scr[...] = r2_ref[...].astype(BF16)
        e2_scr[...] = e2_ref[...].astype(BF16)

    def bcast_bf16(row):
        return jnp.tile(jnp.broadcast_to(row, (16, 128)).astype(BF16), (HEAD_DIM // 16, 1))

    nsub = PEER_SUB // HEAD_DIM
    igrp = min(nsub, 4)
    nchunk = ec // PEER_SUB

    def activations(sc):
        es = slice(sc * PEER_SUB, (sc + 1) * PEER_SUB)
        act_scr[es, :] = _gelu(_dot(u_ref[es, :], bt_ref[...])).astype(BF16)

    def gates(sc):
        for tl in range(t // 128):
            ls = slice(tl * 128, (tl + 1) * 128)
            for ig in range(nsub // igrp):
                w = [jnp.zeros((HEAD_DIM, 128), BF16) for _ in range(igrp)]
                for h in range(PEER_HEADS):
                    hs = slice(h * HEAD_DIM, (h + 1) * HEAD_DIM)
                    rank2 = r2_scr[hs, ls]
                    e2 = e2_scr[hs, ls]
                    for ii in range(igrp):
                        il = sc * nsub + ig * igrp + ii
                        limit1 = bcast_bf16(l1_ref[h, il:il + 1, ls])
                        e1 = bcast_bf16(e1_ref[h, il:il + 1, ls])
                        w[ii] = w[ii] + jnp.where(rank2 < limit1, e2 * e1, 0)
                for ii in range(igrp):
                    il = sc * nsub + ig * igrp + ii
                    rs = slice(il * HEAD_DIM, (il + 1) * HEAD_DIM)
                    wa_scr[rs, ls] = w[ii] * act_scr[rs, ls]

    def values(sc):
        es = slice(sc * PEER_SUB, (sc + 1) * PEER_SUB)
        acc_ref[...] += lax.dot_general(vt_ref[es, :], wa_scr[es, :], (((0,), (0,)), ((), ())),
                                        preferred_element_type=F32)

    activations(0)
    for sc in range(nchunk):
        if sc + 1 < nchunk:
            activations(sc + 1)
        gates(sc)
        values(sc)

    @pl.when(k == pl.num_programs(1) - 1)
    def _():
        out = acc_ref[...].T
        for g in range(groups):
            r = slice(g * rows, (g + 1) * rows)
            h2 = h1_ref[r, :] + mod_ref[g, 5:6, :] * out[r, :]
            y_ref[r, :] = _rms(h2) * gf_ref[...]


def _peer(bt, u, vt, l1, e1, r2, e2, h1, mod, gf, rows_per_batch, t, ec):
    d, n = bt.shape
    ne = u.shape[0]
    nr = r2.shape[0]
    groups, mod_spec = _mod_blockspec(t, rows_per_batch, d, 2)
    spec_i = pl.BlockSpec((PEER_HEADS, ec // HEAD_DIM, t), lambda i, k: (0, k, i))
    spec_j = pl.BlockSpec((nr, t), lambda i, k: (0, i))
    return pl.pallas_call(
        functools.partial(_peer_kernel, groups=groups, rows=t // groups, ec=ec),
        out_shape=jax.ShapeDtypeStruct((n, d), F32),
        grid=(n // t, ne // ec),
        in_specs=[pl.BlockSpec((d, t), lambda i, k: (0, i)),
                  pl.BlockSpec((ec, d), lambda i, k: (k, 0)),
                  pl.BlockSpec((ec, d), lambda i, k: (k, 0)),
                  spec_j, spec_j, spec_i, spec_i,
                  pl.BlockSpec((t, d), lambda i, k: (i, 0)),
                  mod_spec,
                  pl.BlockSpec((1, d), lambda i, k: (0, 0))],
        out_specs=pl.BlockSpec((t, d), lambda i, k: (i, 0)),
        scratch_shapes=[pltpu.VMEM((d, t), F32), pltpu.VMEM((ec, t), BF16), pltpu.VMEM((ec, t), BF16),
                        pltpu.VMEM((nr, t), BF16), pltpu.VMEM((nr, t), BF16)],
        compiler_params=_cparams("arbitrary", "arbitrary"),
        name="peer_dense",
    )(bt, u, vt, r2, e2, l1, e1, h1, mod, gf)


def _rope_tables(pos):
    half = HEAD_DIM // 2
    inv_freq = ROPE_BASE ** (-jnp.arange(half, dtype=F32) / half)
    ang = pos.astype(F32)[:, None] * inv_freq[None, :]
    cos, sin = jnp.cos(ang), jnp.sin(ang)
    return jnp.concatenate([cos, cos], axis=1), jnp.concatenate([-sin, sin], axis=1)


def _tile(n, pref):
    t = min(n, pref)
    assert n % t == 0
    return t


def _run_group(x, mod, pos, s0, k_past, v_past, params):
    (g1, w_in, ret_g, w_out, g2, wq_t, keys, u, vt, gf) = params
    nbatch, seq, d = x.shape
    n = nbatch * seq
    x2 = x.reshape(n, d)

    tm = _tile(n, 512)
    reps = max(1, tm // seq)
    cos_t, sin_t = _rope_tables(pos)
    cos_t, sin_t = jnp.tile(cos_t, (reps, 1)), jnp.tile(sin_t, (reps, 1))
    rq, rk, rv, rg, sq, sk, sv = _inproj(x2, mod, g1, w_in, cos_t, sin_t, seq, tm)

    chunk = min(seq, RET_CHUNK)
    ret, s_new = _retention(rq, rk, rv, rg, s0, ret_g, nbatch, seq, chunk)

    if k_past is None:
        so = _stick_breaking_prompt(sq, sk, sv, nbatch, seq)
    else:
        so = _stick_breaking_sample(sq, sk, sv, k_past, v_past, nbatch, seq)

    tp = _tile(n, 512)
    h1, bt, st = _post(x2, ret, so, mod, w_out, g2, wq_t, keys, seq, tp)
    l1, e1, r2, e2 = _topk_stats(st, _tile(n, 256))
    y = _peer(bt, u, vt, l1, e1, r2, e2, h1, mod, gf, seq, tp, 2048)
    return (y.reshape(nbatch, seq, d),
            s_new.reshape(1, nbatch, RET_HEADS, HEAD_DIM, HEAD_DIM),
            sk.reshape(1, nbatch, seq, SB_HEADS, HEAD_DIM),
            sv.reshape(1, nbatch, seq, SB_HEADS, HEAD_DIM))


def kernel(x_prompt, x_sample, c_prompt, c_sample, state_ret, cache_sb_k, cache_sb_v, w_ada, b_ada, norm1_g,
           w_in, ret_norm_g, w_out, norm2_g, peer_w_query, peer_sub_keys, peer_u, peer_v, final_norm_g):
    assert w_ada.shape[0] == 1, "single-layer step"
    bp, seq_p, d = x_prompt.shape
    bs, seq_s, _ = x_sample.shape
    past_len = cache_sb_k.shape[2]

    mod = _adaln(jnp.concatenate([c_prompt, c_sample], axis=0), w_ada[0], b_ada[0]).reshape(bp + bs, 6, d)
    params = (norm1_g[0].reshape(1, d), w_in[0].astype(BF16), ret_norm_g[0].reshape(1, MIX_W),
              w_out[0].astype(BF16), norm2_g[0].reshape(1, d), peer_w_query[0].T.astype(BF16),
              peer_sub_keys[0].astype(BF16), peer_u[0].astype(BF16), peer_v[0].astype(BF16),
              final_norm_g.reshape(1, d))

    zeros_state = jnp.zeros((bp, RET_HEADS, HEAD_DIM, HEAD_DIM), F32)
    y_p, s_p, k_p, v_p = _run_group(x_prompt, mod[:bp], jnp.arange(seq_p), zeros_state, None, None, params)
    y_s, s_s, k_s, v_s = _run_group(x_sample, mod[bp:], past_len + jnp.arange(seq_s), state_ret[0],
                                    cache_sb_k, cache_sb_v, params)
    return (y_p, y_s, s_p, k_p, v_p, s_s, k_s, v_s)
```

```python
import functools

import jax
import jax.numpy as jnp
from jax import lax
from jax.experimental import pallas as pl
from jax.experimental.pallas import tpu as pltpu

F32 = jnp.float32
BF16 = jnp.bfloat16

NORM_EPS = 1e-6
ROPE_BASE = 10000.0
HEAD_DIM = 128
RET_HEADS = 4
SB_HEADS = 4
MIX_W = RET_HEADS * HEAD_DIM
PEER_HEADS = 8
PEER_TOPK = 16
SB_QBLOCK = 512
RET_CHUNK = 512
VMEM_LIMIT = 56 * 1024 * 1024
NEG_INF = float("-inf")


def _cparams(*sem):
    return pltpu.CompilerParams(dimension_semantics=sem, vmem_limit_bytes=VMEM_LIMIT)


def _dot(a, b):
    return jnp.dot(a, b, preferred_element_type=F32)


def _dot_nt(a, b):
    return lax.dot_general(a, b, (((1,), (1,)), ((), ())), preferred_element_type=F32)


def _split_bf16(x):
    hi = x.astype(BF16)
    lo = (x - hi.astype(F32)).astype(BF16)
    return hi, lo


def _rms(x):
    return x * lax.rsqrt(jnp.mean(x * x, axis=-1, keepdims=True) + NORM_EPS)


def _silu(x):
    return x / (1.0 + jnp.exp(-x))


def _adaln_kernel(c_ref, w_ref, b_ref, o_ref):
    s = _silu(c_ref[...])
    s_hi, s_lo = _split_bf16(s)
    w_hi, w_lo = _split_bf16(w_ref[...])
    o_ref[...] = _dot(s_hi, w_hi) + _dot(s_hi, w_lo) + _dot(s_lo, w_hi) + b_ref[...]


def _adaln(c, w_ada, b_ada):
    nb, d = c.shape
    n = w_ada.shape[1]
    tn = 1024
    return pl.pallas_call(
        _adaln_kernel,
        out_shape=jax.ShapeDtypeStruct((nb, n), F32),
        grid=(n // tn,),
        in_specs=[pl.BlockSpec((nb, d), lambda j: (0, 0)),
                  pl.BlockSpec((d, tn), lambda j: (0, j)),
                  pl.BlockSpec((1, tn), lambda j: (0, j))],
        out_specs=pl.BlockSpec((nb, tn), lambda j: (0, j)),
        compiler_params=_cparams("arbitrary"),
        name="adaln",
    )(c, w_ada, b_ada.reshape(1, n))


def _mod_blockspec(tm, rows_per_batch, d, ngrid):
    groups = max(1, tm // rows_per_batch)
    if groups == 1:
        per = rows_per_batch // tm
        if ngrid == 1:
            return groups, pl.BlockSpec((1, 6, d), lambda i: (i // per, 0, 0))
        return groups, pl.BlockSpec((1, 6, d), lambda i, k: (i // per, 0, 0))
    if ngrid == 1:
        return groups, pl.BlockSpec((groups, 6, d), lambda i: (i, 0, 0))
    return groups, pl.BlockSpec((groups, 6, d), lambda i, k: (i, 0, 0))


def _inproj_kernel(x_ref, mod_ref, g1_ref, w_ref, cos_ref, sin_ref,
                   rq_ref, rk_ref, rv_ref, rg_ref, sq_ref, sk_ref, sv_ref, a_scr, *, groups, rows):
    for g in range(groups):
        r = slice(g * rows, (g + 1) * rows)
        xn = _rms(x_ref[r, :]) * g1_ref[...]
        a = xn * (1.0 + mod_ref[g, 1:2, :]) + mod_ref[g, 0:1, :]
        a_scr[r, :] = a.astype(BF16)
    y = _dot(a_scr[...], w_ref[...])
    cos = cos_ref[...]
    sin = sin_ref[...]
    for h in range(RET_HEADS):
        c = slice(h * HEAD_DIM, (h + 1) * HEAD_DIM)
        yq = y[:, h * HEAD_DIM:(h + 1) * HEAD_DIM]
        yk = y[:, MIX_W + h * HEAD_DIM:MIX_W + (h + 1) * HEAD_DIM]
        rq = (yq * cos + pltpu.roll(yq, HEAD_DIM // 2, 1) * sin) * (HEAD_DIM ** -0.5)
        rk = yk * cos + pltpu.roll(yk, HEAD_DIM // 2, 1) * sin
        rq_ref[:, c] = rq.astype(rq_ref.dtype)
        rk_ref[:, c] = rk.astype(rk_ref.dtype)
    rv_ref[...] = y[:, 2 * MIX_W:3 * MIX_W].astype(rv_ref.dtype)
    rg_ref[...] = y[:, 3 * MIX_W:4 * MIX_W].astype(rg_ref.dtype)
    sq_ref[...] = y[:, 4 * MIX_W:5 * MIX_W].astype(sq_ref.dtype)
    sk_ref[...] = y[:, 5 * MIX_W:6 * MIX_W]
    sv_ref[...] = y[:, 6 * MIX_W:7 * MIX_W]


def _inproj(x, mod, g1, w_in, cos_t, sin_t, rows_per_batch, tm):
    n, d = x.shape
    ncol = w_in.shape[1]
    groups, mod_spec = _mod_blockspec(tm, rows_per_batch, d, 1)
    npos = cos_t.shape[0] // tm
    tok = lambda dt: jax.ShapeDtypeStruct((n, MIX_W), dt)
    tspec = pl.BlockSpec((tm, MIX_W), lambda i: (i, 0))
    return pl.pallas_call(
        functools.partial(_inproj_kernel, groups=groups, rows=tm // groups),
        out_shape=(tok(BF16), tok(BF16), tok(BF16), tok(BF16), tok(BF16), tok(F32), tok(F32)),
        grid=(n // tm,),
        in_specs=[pl.BlockSpec((tm, d), lambda i: (i, 0)),
                  mod_spec,
                  pl.BlockSpec((1, d), lambda i: (0, 0)),
                  pl.BlockSpec((d, ncol), lambda i: (0, 0)),
                  pl.BlockSpec((tm, HEAD_DIM), lambda i: (i % npos, 0)),
                  pl.BlockSpec((tm, HEAD_DIM), lambda i: (i % npos, 0))],
        out_specs=(tspec,) * 7,
        scratch_shapes=[pltpu.VMEM((tm, d), BF16)],
        compiler_params=_cparams("arbitrary"),
        name="inproj",
    )(x, mod, g1, w_in, cos_t, sin_t)


def _retention_kernel(q_ref, k_ref, v_ref, g_ref, s0_ref, intra_ref, qd_ref, kd_ref, cd_ref, gn_ref,
                      o_ref, sout_ref, s_scr):
    c = pl.program_id(1)

    @pl.when(c == 0)
    def _():
        s_scr[...] = s0_ref[0]

    for h in range(RET_HEADS):
        cs = slice(h * HEAD_DIM, (h + 1) * HEAD_DIM)
        q = q_ref[:, cs]
        k = k_ref[:, cs]
        v = v_ref[:, cs]
        state = s_scr[h]
        scores = _dot_nt(q, k) * intra_ref[h]
        o = _dot(scores.astype(BF16), v) + _dot(q, state.astype(BF16)) * qd_ref[h]
        kdt = (k.astype(F32) * kd_ref[h]).T.astype(BF16)
        s_scr[h] = cd_ref[h] * state + _dot(kdt, v)
        cen = o - jnp.mean(o, axis=-1, keepdims=True)
        on = cen * lax.rsqrt(jnp.mean(cen * cen, axis=-1, keepdims=True) + NORM_EPS)
        gate = _silu(g_ref[:, cs].astype(F32))
        o_ref[:, cs] = (on * gn_ref[:, cs] * gate).astype(o_ref.dtype)

    @pl.when(c == pl.num_programs(1) - 1)
    def _():
        sout_ref[0] = s_scr[...]


def _retention_tables(chunk):
    log_g = jnp.log1p(-jnp.exp2(-5.0 - jnp.arange(RET_HEADS, dtype=F32)))
    idx = jnp.arange(chunk, dtype=F32)
    diff = idx[:, None] - idx[None, :]
    causal = diff >= 0
    intra = jnp.where(causal[None], jnp.exp(jnp.where(causal, diff, 0.0)[None] * log_g[:, None, None]), 0.0)
    qd = jnp.exp((idx[None, :] + 1.0) * log_g[:, None])
    kd = jnp.exp((chunk - 1.0 - idx)[None, :] * log_g[:, None])
    cd = jnp.exp(chunk * log_g)
    bc = lambda t: jnp.broadcast_to(t[:, :, None], (RET_HEADS, chunk, HEAD_DIM))
    return intra, bc(qd), bc(kd), jnp.broadcast_to(cd[:, None, None], (RET_HEADS, 1, HEAD_DIM))


def _retention(rq, rk, rv, rg, s0, ret_norm_g, nbatch, seq, chunk):
    n = rq.shape[0]
    nc = seq // chunk
    intra, qd, kd, cd = _retention_tables(chunk)
    tspec = pl.BlockSpec((chunk, MIX_W), lambda b, c: (b * nc + c, 0))
    full = lambda shape: pl.BlockSpec(shape, lambda b, c: (0,) * len(shape))
    sspec = pl.BlockSpec((1, RET_HEADS, HEAD_DIM, HEAD_DIM), lambda b, c: (b, 0, 0, 0))
    return pl.pallas_call(
        _retention_kernel,
        out_shape=(jax.ShapeDtypeStruct((n, MIX_W), BF16),
                   jax.ShapeDtypeStruct((nbatch, RET_HEADS, HEAD_DIM, HEAD_DIM), F32)),
        grid=(nbatch, nc),
        in_specs=[tspec, tspec, tspec, tspec, sspec,
                  full((RET_HEADS, chunk, chunk)), full((RET_HEADS, chunk, HEAD_DIM)),
                  full((RET_HEADS, chunk, HEAD_DIM)), full((RET_HEADS, 1, HEAD_DIM)), full((1, MIX_W))],
        out_specs=(tspec, sspec),
        scratch_shapes=[pltpu.VMEM((RET_HEADS, HEAD_DIM, HEAD_DIM), F32)],
        compiler_params=_cparams("arbitrary", "arbitrary"),
        name="retention",
    )(rq, rk, rv, rg, s0, intra, qd, kd, cd, ret_norm_g)


def _log_sigmoid(z):
    return jnp.minimum(z, 0.0) - jnp.log(1.0 + jnp.exp(-jnp.abs(z)))


SB_TILE = 512
SB_CACHE_TILE = 2048


def _sb_scores(q, kblk):
    return _dot_nt(q, kblk) * (HEAD_DIM ** -0.5)


def _sb_cumulate(z, cumw, mask):
    nsub = z.shape[1] // 128
    log_beta = _log_sigmoid(z)
    log_keep = log_beta - z
    if mask is not None:
        log_keep = jnp.where(mask, log_keep, 0.0)
    parts = _split_bf16(log_keep)
    lhs = jnp.concatenate(
        [jnp.concatenate([p[:, c * 128:(c + 1) * 128] for p in parts], axis=1) for c in range(nsub)], axis=0)
    return log_beta, _dot(lhs, cumw)


def _sb_weights(log_beta, cw, vblk, carry, mask):
    nsub = vblk.shape[0] // 128
    qb = cw.shape[0] // nsub
    a = [None] * nsub
    for c in reversed(range(nsub)):
        a[c] = jnp.exp(log_beta[:, c * 128:(c + 1) * 128] + (cw[c * qb:(c + 1) * qb, 0:128] + carry))
        carry = carry + cw[c * qb:(c + 1) * qb, 128:256]
    a = jnp.concatenate(a, axis=1)
    if mask is not None:
        a = jnp.where(mask, a, 0.0)
    return _dot(a.astype(BF16), vblk), carry


def _sb_tiles(q_ref, k_of, v_of, cumw_ref, carry_of, mask):
    heads = _sb_heads()
    z, cw, res = {}, {}, {}
    for step in range(len(heads) + 2):
        if step < len(heads):
            z[step] = _sb_scores(q_ref[:, heads[step]], k_of(heads[step]))
        if 0 <= step - 1 < len(heads):
            cw[step - 1] = _sb_cumulate(z.pop(step - 1), cumw_ref[...], mask)
        if 0 <= step - 2 < len(heads):
            hs = heads[step - 2]
            res[step - 2] = _sb_weights(*cw.pop(step - 2), v_of(hs), carry_of(hs), mask)
    return [res[h] for h in range(len(heads))]


def _sb_heads():
    return [slice(h * HEAD_DIM, (h + 1) * HEAD_DIM) for h in range(SB_HEADS)]


def _sb_past_tiles(q_ref, kp_ref, vp_ref, cumw_ref, out_scr, carry_scr, ntiles):
    def body(it, _):
        start = pl.multiple_of((ntiles - 1 - it) * SB_TILE, SB_TILE)
        res = _sb_tiles(q_ref, lambda hs: kp_ref[0, pl.ds(start, SB_TILE), hs].astype(BF16),
                        lambda hs: vp_ref[0, pl.ds(start, SB_TILE), hs].astype(BF16),
                        cumw_ref, lambda hs: carry_scr[:, hs], None)
        for hs, (o, carry) in zip(_sb_heads(), res):
            out_scr[:, hs] += o
            carry_scr[:, hs] = carry
        return 0

    lax.fori_loop(0, ntiles, body, 0)


def _sb_prompt_kernel(q_ref, kp_ref, vp_ref, cumw_ref, o_ref, out_scr, carry_scr):
    i = pl.program_id(1)
    qb = q_ref.shape[0]
    last = (i * qb) // SB_TILE
    start = pl.multiple_of(last * SB_TILE, SB_TILE)
    q_pos = i * qb + lax.broadcasted_iota(jnp.int32, (qb, SB_TILE), 0)
    k_pos = start + lax.broadcasted_iota(jnp.int32, (qb, SB_TILE), 1)
    res = _sb_tiles(q_ref, lambda hs: kp_ref[0, pl.ds(start, SB_TILE), hs].astype(BF16),
                    lambda hs: vp_ref[0, pl.ds(start, SB_TILE), hs].astype(BF16),
                    cumw_ref, lambda hs: jnp.zeros((qb, 128), F32), k_pos < q_pos)
    for hs, (o, carry) in zip(_sb_heads(), res):
        out_scr[:, hs] = o
        carry_scr[:, hs] = carry
    _sb_past_tiles(q_ref, kp_ref, vp_ref, cumw_ref, out_scr, carry_scr, last)
    o_ref[...] = out_scr[...].astype(o_ref.dtype)


def _sb_sample_kernel(q_ref, kn_ref, vn_ref, kc_hbm, vc_hbm, cumw_ref, o_ref,
                      kpad, vpad, out_scr, carry_scr, kbuf, vbuf, sem):
    b = pl.program_id(0)
    qb = q_ref.shape[0]
    tile = kbuf.shape[2]
    ntiles = kc_hbm.shape[2] // tile
    head_of = {hs.start: h for h, hs in enumerate(_sb_heads())}

    def tile_copies(it, slot):
        start = pl.multiple_of((ntiles - 1 - it) * tile, tile)
        cps = []
        for h in range(SB_HEADS):
            cps.append(pltpu.make_async_copy(kc_hbm.at[0, b, pl.ds(start, tile), h, :],
                                             kbuf.at[slot, h], sem.at[0, slot, h]))
            cps.append(pltpu.make_async_copy(vc_hbm.at[0, b, pl.ds(start, tile), h, :],
                                             vbuf.at[slot, h], sem.at[1, slot, h]))
        return cps

    for cp in tile_copies(0, 0):
        cp.start()

    kpad[...] = jnp.zeros_like(kpad)
    vpad[...] = jnp.zeros_like(vpad)
    kpad[0:qb, :] = kn_ref[...]
    vpad[0:qb, :] = vn_ref[...]
    tq = lax.broadcasted_iota(jnp.int32, (qb, 128), 0)
    ts = lax.broadcasted_iota(jnp.int32, (qb, 128), 1)
    res = _sb_tiles(q_ref, lambda hs: kpad[:, hs].astype(BF16), lambda hs: vpad[:, hs].astype(BF16),
                    cumw_ref, lambda hs: jnp.zeros((qb, 128), F32), ts < tq)
    for hs, (o, carry) in zip(_sb_heads(), res):
        out_scr[:, hs] = o
        carry_scr[:, hs] = carry

    def body(it, _):
        slot = it % 2

        @pl.when(it + 1 < ntiles)
        def _():
            for cp in tile_copies(it + 1, 1 - slot):
                cp.start()

        for cp in tile_copies(it, slot):
            cp.wait()
        res = _sb_tiles(q_ref, lambda hs: kbuf[slot, head_of[hs.start]].astype(BF16),
                        lambda hs: vbuf[slot, head_of[hs.start]].astype(BF16),
                        cumw_ref, lambda hs: carry_scr[:, hs], None)
        for hs, (o, carry) in zip(_sb_heads(), res):
            out_scr[:, hs] += o
            carry_scr[:, hs] = carry
        return 0

    lax.fori_loop(0, ntiles, body, 0)
    o_ref[...] = out_scr[...].astype(o_ref.dtype)


def _cumw():
    n = 128
    later = lax.broadcasted_iota(jnp.int32, (n, n), 0) > lax.broadcasted_iota(jnp.int32, (n, n), 1)
    half = jnp.concatenate([later.astype(F32), jnp.ones((n, n), F32)], axis=1)
    return jnp.concatenate([half, half], axis=0).astype(BF16)


def _stick_breaking_prompt(sq, sk, sv, nbatch, seq):
    n = sq.shape[0]
    qb = SB_QBLOCK
    nq = seq // qb
    assert seq % SB_TILE == 0
    tspec = pl.BlockSpec((qb, MIX_W), lambda b, i: (b * nq + i, 0))
    pspec = pl.BlockSpec((1, seq, MIX_W), lambda b, i: (b, 0, 0))
    return pl.pallas_call(
        _sb_prompt_kernel,
        out_shape=jax.ShapeDtypeStruct((n, MIX_W), BF16),
        grid=(nbatch, nq),
        in_specs=[tspec, pspec, pspec, pl.BlockSpec((256, 256), lambda b, i: (0, 0))],
        out_specs=tspec,
        scratch_shapes=[pltpu.VMEM((qb, MIX_W), F32), pltpu.VMEM((qb, MIX_W), F32)],
        compiler_params=_cparams("arbitrary", "arbitrary"),
        name="stick_breaking",
    )(sq, sk.reshape(nbatch, seq, MIX_W), sv.reshape(nbatch, seq, MIX_W), _cumw())


def _stick_breaking_sample(sq, sk, sv, k_cache, v_cache, nbatch, seq):
    n = sq.shape[0]
    past_len = k_cache.shape[2]
    tile = SB_CACHE_TILE if past_len % SB_CACHE_TILE == 0 else SB_TILE
    assert seq <= 128 and past_len % tile == 0
    tspec = pl.BlockSpec((seq, MIX_W), lambda b: (b, 0))
    hbm = pl.BlockSpec(memory_space=pl.ANY)
    return pl.pallas_call(
        _sb_sample_kernel,
        out_shape=jax.ShapeDtypeStruct((n, MIX_W), BF16),
        grid=(nbatch,),
        in_specs=[tspec, tspec, tspec, hbm, hbm, pl.BlockSpec((256, 256), lambda b: (0, 0))],
        out_specs=tspec,
        scratch_shapes=[pltpu.VMEM((128, MIX_W), F32), pltpu.VMEM((128, MIX_W), F32),
                        pltpu.VMEM((seq, MIX_W), F32), pltpu.VMEM((seq, MIX_W), F32),
                        pltpu.VMEM((2, SB_HEADS, tile, HEAD_DIM), F32),
                        pltpu.VMEM((2, SB_HEADS, tile, HEAD_DIM), F32),
                        pltpu.SemaphoreType.DMA((2, 2, SB_HEADS))],
        compiler_params=_cparams("arbitrary"),
        name="stick_breaking_cached",
    )(sq, sk, sv, k_cache, v_cache, _cumw())


def _post_kernel(x_ref, ret_ref, so_ref, mod_ref, wo_ref, g2_ref, wqt_ref, keys_ref,
                 h1_ref, bt_ref, st_ref, b_scr, *, groups, rows):
    attn = _dot(ret_ref[...], wo_ref[0:MIX_W, :]) + _dot(so_ref[...], wo_ref[MIX_W:2 * MIX_W, :])
    for g in range(groups):
        r = slice(g * rows, (g + 1) * rows)
        h1 = x_ref[r, :] + mod_ref[g, 2:3, :] * attn[r, :]
        h1_ref[r, :] = h1
        b_scr[r, :] = _rms(h1) * g2_ref[...] * (1.0 + mod_ref[g, 4:5, :]) + mod_ref[g, 3:4, :]
    bt = b_scr[...].T.astype(BF16)
    bt_ref[...] = bt
    qt = _dot(wqt_ref[...], bt).astype(BF16)
    for hp in range(2 * PEER_HEADS):
        rs = slice(hp * HEAD_DIM, (hp + 1) * HEAD_DIM)
        st_ref[rs, :] = _dot(keys_ref[hp % 2, hp // 2], qt[rs, :])


def _post(x, ret, so, mod, w_out, g2, wq_t, keys, rows_per_batch, tm):
    n, d = x.shape
    nq = wq_t.shape[0]
    groups, mod_spec = _mod_blockspec(tm, rows_per_batch, d, 1)
    return pl.pallas_call(
        functools.partial(_post_kernel, groups=groups, rows=tm // groups),
        out_shape=(jax.ShapeDtypeStruct((n, d), F32),
                   jax.ShapeDtypeStruct((d, n), BF16),
                   jax.ShapeDtypeStruct((nq, n), F32)),
        grid=(n // tm,),
        in_specs=[pl.BlockSpec((tm, d), lambda i: (i, 0)),
                  pl.BlockSpec((tm, MIX_W), lambda i: (i, 0)),
                  pl.BlockSpec((tm, MIX_W), lambda i: (i, 0)),
                  mod_spec,
                  pl.BlockSpec((2 * MIX_W, d), lambda i: (0, 0)),
                  pl.BlockSpec((1, d), lambda i: (0, 0)),
                  pl.BlockSpec((nq, d), lambda i: (0, 0)),
                  pl.BlockSpec(keys.shape, lambda i: (0, 0, 0, 0))],
        out_specs=(pl.BlockSpec((tm, d), lambda i: (i, 0)),
                   pl.BlockSpec((d, tm), lambda i: (0, i)),
                   pl.BlockSpec((nq, tm), lambda i: (0, i))),
        scratch_shapes=[pltpu.VMEM((tm, d), F32)],
        compiler_params=_cparams("arbitrary"),
        name="post_mix",
    )(x, ret, so, mod, w_out, g2, wq_t, keys)


def _stack_rows(rows):
    t = rows[0].shape[1]
    ridx = lax.broadcasted_iota(jnp.int32, (8, t), 0)
    out = jnp.broadcast_to(rows[0], (8, t))
    for r in range(1, len(rows)):
        out = jnp.where(ridx == r, rows[r], out)
    return out


def _top_values(x, count):
    vals = []
    for _ in range(count):
        m = jnp.max(x, axis=0, keepdims=True)
        vals.append(m)
        x = jnp.where(x == m, NEG_INF, x)
    return vals


def _bitonic_pairs(n):
    pairs = []
    k = 2
    while k <= n:
        j = k // 2
        while j >= 1:
            for i in range(n):
                l = i ^ j
                if l > i:
                    pairs.append((i, l, (i & k) == 0))
            j //= 2
        k *= 2
    return pairs


def _top16_of_128(x):
    c = [x[8 * v:8 * (v + 1), :] for v in range(16)]
    for i, j, desc in _bitonic_pairs(16):
        hi, lo = jnp.maximum(c[i], c[j]), jnp.minimum(c[i], c[j])
        c[i], c[j] = (hi, lo) if desc else (lo, hi)
    vals = []
    for r in range(PEER_TOPK):
        m = jnp.max(c[0], axis=0, keepdims=True)
        vals.append(m)
        if r + 1 < PEER_TOPK:
            hit = c[0] == m
            depth = PEER_TOPK - r
            for k in range(depth - 1):
                c[k] = jnp.where(hit, c[k + 1], c[k])
            c[depth - 1] = jnp.where(hit, NEG_INF, c[depth - 1])
    return vals


def _count_prefix(test, rows):
    t1 = test(rows[7])
    t2 = test(jnp.where(t1, rows[11], rows[3]))
    lo3 = jnp.where(t2, rows[5], rows[1])
    hi3 = jnp.where(t2, rows[13], rows[9])
    t3 = test(jnp.where(t1, hi3, lo3))
    c00 = jnp.where(t3, rows[2], rows[0])
    c01 = jnp.where(t3, rows[6], rows[4])
    c10 = jnp.where(t3, rows[10], rows[8])
    c11 = jnp.where(t3, rows[14], rows[12])
    t4 = test(jnp.where(t1, jnp.where(t2, c11, c10), jnp.where(t2, c01, c00)))
    t5 = test(rows[15])
    count = jnp.where(t1, 8.0, 0.0) + jnp.where(t2, 4.0, 0.0) + jnp.where(t3, 2.0, 0.0)
    return count + jnp.where(t4, 1.0, 0.0) + jnp.where(t5, 1.0, 0.0)


def _topk_kernel(s_ref, l1_ref, e1_ref, r2_ref, e2_ref):
    tt = s_ref.shape[1]
    ridx = lax.broadcasted_iota(jnp.int32, (8, tt), 0)
    for h in range(PEER_HEADS):
        base = 2 * h * HEAD_DIM
        s1 = s_ref[base:base + HEAD_DIM, :]
        s2 = s_ref[base + HEAD_DIM:base + 2 * HEAD_DIM, :]
        a = _top16_of_128(s1)
        b = _top16_of_128(s2)
        a_lo, a_hi = _stack_rows(a[0:8]), _stack_rows(a[8:16])
        b_lo, b_hi = _stack_rows(b[0:8]), _stack_rows(b[8:16])
        b_mid = jnp.where(ridx >= 5, b_lo, NEG_INF)
        cand = jnp.concatenate(
            [a_lo + b[l] for l in range(5)] + [a_hi + b[0], b_hi + a[0], b_mid + a[0], b_mid + a[1]], axis=0)
        top = _top_values(cand, PEER_TOPK)
        tau = top[PEER_TOPK - 1]
        z = jnp.zeros_like(tau)
        for v in top:
            z = z + jnp.exp(v - top[0])
        l1_ref[h] = _count_prefix(lambda row: s1 + row >= tau, b)
        e1_ref[h] = jnp.exp(s1 - a[0]) / z
        r2_ref[h * HEAD_DIM:(h + 1) * HEAD_DIM, :] = _count_prefix(lambda row: row > s2, b)
        e2_ref[h * HEAD_DIM:(h + 1) * HEAD_DIM, :] = jnp.exp(s2 - b[0])


def _topk_stats(st, tt):
    nq, n = st.shape
    rows = PEER_HEADS * HEAD_DIM
    spec3 = pl.BlockSpec((PEER_HEADS, HEAD_DIM, tt), lambda i: (0, 0, i))
    spec2 = pl.BlockSpec((rows, tt), lambda i: (0, i))
    return pl.pallas_call(
        _topk_kernel,
        out_shape=(jax.ShapeDtypeStruct((PEER_HEADS, HEAD_DIM, n), F32),
                   jax.ShapeDtypeStruct((PEER_HEADS, HEAD_DIM, n), F32),
                   jax.ShapeDtypeStruct((rows, n), F32),
                   jax.ShapeDtypeStruct((rows, n), F32)),
        grid=(n // tt,),
        in_specs=[pl.BlockSpec((nq, tt), lambda i: (0, i))],
        out_specs=(spec3, spec3, spec2, spec2),
        compiler_params=_cparams("arbitrary"),
        name="peer_topk",
    )(st)


PEER_SUB = 1024
PEER_STEP = 2048
TOKEN_TILE = 512
TOPK_TILE = 256


def _gelu(x):
    half = 0.5 * x
    inner = x * (0.7978845608028654 + (0.7978845608028654 * 0.044715) * (x * x))
    return half + half * jnp.tanh(inner)


def _peer_kernel(bt_ref, u_ref, vt_ref, r2_ref, e2_ref, l1_ref, e1_ref, h1_ref, mod_ref, gf_ref, y_ref,
                 acc_ref, wa_scr, act_scr, r2_scr, e2_scr, *, groups, rows, ec):
    k = pl.program_id(1)
    t = bt_ref.shape[1]

    @pl.when(k == 0)
    def _():
        acc_ref[...] = jnp.zeros_like(acc_ref)
        r2_scr[...] = r2_ref[...].astype(BF16)
        e2_scr[...] = e2_ref[...].astype(BF16)

    def bcast_bf16(row):
        return jnp.tile(jnp.broadcast_to(row, (16, 128)).astype(BF16), (HEAD_DIM // 16, 1))

    nsub = PEER_SUB // HEAD_DIM
    igrp = min(nsub, 4)
    nchunk = ec // PEER_SUB

    def activations(sc):
        es = slice(sc * PEER_SUB, (sc + 1) * PEER_SUB)
        act_scr[es, :] = _gelu(_dot(u_ref[es, :], bt_ref[...])).astype(BF16)

    def gates(sc):
        for tl in range(t // 128):
            ls = slice(tl * 128, (tl + 1) * 128)
            for ig in range(nsub // igrp):
                w = [jnp.zeros((HEAD_DIM, 128), BF16) for _ in range(igrp)]
                for h in range(PEER_HEADS):
                    hs = slice(h * HEAD_DIM, (h + 1) * HEAD_DIM)
                    rank2 = r2_scr[hs, ls]
                    e2 = e2_scr[hs, ls]
                    for ii in range(igrp):
                        il = sc * nsub + ig * igrp + ii
                        limit1 = bcast_bf16(l1_ref[h, il:il + 1, ls])
                        e1 = bcast_bf16(e1_ref[h, il:il + 1, ls])
                        w[ii] = w[ii] + jnp.where(rank2 < limit1, e2 * e1, 0)
                for ii in range(igrp):
                    il = sc * nsub + ig * igrp + ii
                    rs = slice(il * HEAD_DIM, (il + 1) * HEAD_DIM)
                    wa_scr[rs, ls] = w[ii] * act_scr[rs, ls]

    def values(sc):
        es = slice(sc * PEER_SUB, (sc + 1) * PEER_SUB)
        acc_ref[...] += lax.dot_general(vt_ref[es, :], wa_scr[es, :], (((0,), (0,)), ((), ())),
                                        preferred_element_type=F32)

    activations(0)
    for sc in range(nchunk):
        if sc + 1 < nchunk:
            activations(sc + 1)
        gates(sc)
        values(sc)

    @pl.when(k == pl.num_programs(1) - 1)
    def _():
        out = acc_ref[...].T
        for g in range(groups):
            r = slice(g * rows, (g + 1) * rows)
            h2 = h1_ref[r, :] + mod_ref[g, 5:6, :] * out[r, :]
            y_ref[r, :] = _rms(h2) * gf_ref[...]


def _peer(bt, u, vt, l1, e1, r2, e2, h1, mod, gf, rows_per_batch, t, ec):
    d, n = bt.shape
    ne = u.shape[0]
    nr = r2.shape[0]
    groups, mod_spec = _mod_blockspec(t, rows_per_batch, d, 2)
    spec_i = pl.BlockSpec((PEER_HEADS, ec // HEAD_DIM, t), lambda i, k: (0, k, i))
    spec_j = pl.BlockSpec((nr, t), lambda i, k: (0, i))
    return pl.pallas_call(
        functools.partial(_peer_kernel, groups=groups, rows=t // groups, ec=ec),
        out_shape=jax.ShapeDtypeStruct((n, d), F32),
        grid=(n // t, ne // ec),
        in_specs=[pl.BlockSpec((d, t), lambda i, k: (0, i)),
                  pl.BlockSpec((ec, d), lambda i, k: (k, 0)),
                  pl.BlockSpec((ec, d), lambda i, k: (k, 0)),
                  spec_j, spec_j, spec_i, spec_i,
                  pl.BlockSpec((t, d), lambda i, k: (i, 0)),
                  mod_spec,
                  pl.BlockSpec((1, d), lambda i, k: (0, 0))],
        out_specs=pl.BlockSpec((t, d), lambda i, k: (i, 0)),
        scratch_shapes=[pltpu.VMEM((d, t), F32), pltpu.VMEM((ec, t), BF16), pltpu.VMEM((ec, t), BF16),
                        pltpu.VMEM((nr, t), BF16), pltpu.VMEM((nr, t), BF16)],
        compiler_params=_cparams("arbitrary", "arbitrary"),
        name="peer_dense",
    )(bt, u, vt, r2, e2, l1, e1, h1, mod, gf)


def _rope_tables(pos):
    half = HEAD_DIM // 2
    inv_freq = ROPE_BASE ** (-jnp.arange(half, dtype=F32) / half)
    ang = pos.astype(F32)[:, None] * inv_freq[None, :]
    cos, sin = jnp.cos(ang), jnp.sin(ang)
    return jnp.concatenate([cos, cos], axis=1), jnp.concatenate([-sin, sin], axis=1)


def _tile(n, pref):
    t = min(n, pref)
    assert n % t == 0
    return t


def _run_group(x, mod, pos, s0, k_past, v_past, params):
    (g1, w_in, ret_g, w_out, g2, wq_t, keys, u, vt, gf) = params
    nbatch, seq, d = x.shape
    n = nbatch * seq
    x2 = x.reshape(n, d)

    tm = _tile(n, TOKEN_TILE)
    reps = max(1, tm // seq)
    cos_t, sin_t = _rope_tables(pos)
    cos_t, sin_t = jnp.tile(cos_t, (reps, 1)), jnp.tile(sin_t, (reps, 1))
    rq, rk, rv, rg, sq, sk, sv = _inproj(x2, mod, g1, w_in, cos_t, sin_t, seq, tm)

    chunk = min(seq, RET_CHUNK)
    ret, s_new = _retention(rq, rk, rv, rg, s0, ret_g, nbatch, seq, chunk)

    if k_past is None:
        so = _stick_breaking_prompt(sq, sk, sv, nbatch, seq)
    else:
        so = _stick_breaking_sample(sq, sk, sv, k_past, v_past, nbatch, seq)

    tp = _tile(n, TOKEN_TILE)
    h1, bt, st = _post(x2, ret, so, mod, w_out, g2, wq_t, keys, seq, tp)
    l1, e1, r2, e2 = _topk_stats(st, _tile(n, TOPK_TILE))
    y = _peer(bt, u, vt, l1, e1, r2, e2, h1, mod, gf, seq, tp, PEER_STEP)
    return (y.reshape(nbatch, seq, d),
            s_new.reshape(1, nbatch, RET_HEADS, HEAD_DIM, HEAD_DIM),
            sk.reshape(1, nbatch, seq, SB_HEADS, HEAD_DIM),
            sv.reshape(1, nbatch, seq, SB_HEADS, HEAD_DIM))


def kernel(x_prompt, x_sample, c_prompt, c_sample, state_ret, cache_sb_k, cache_sb_v, w_ada, b_ada, norm1_g,
           w_in, ret_norm_g, w_out, norm2_g, peer_w_query, peer_sub_keys, peer_u, peer_v, final_norm_g):
    assert w_ada.shape[0] == 1, "single-layer step"
    bp, seq_p, d = x_prompt.shape
    bs, seq_s, _ = x_sample.shape
    past_len = cache_sb_k.shape[2]

    mod = _adaln(jnp.concatenate([c_prompt, c_sample], axis=0), w_ada[0], b_ada[0]).reshape(bp + bs, 6, d)
    params = (norm1_g[0].reshape(1, d), w_in[0].astype(BF16), ret_norm_g[0].reshape(1, MIX_W),
              w_out[0].astype(BF16), norm2_g[0].reshape(1, d), peer_w_query[0].T.astype(BF16),
              peer_sub_keys[0].astype(BF16), peer_u[0].astype(BF16), peer_v[0].astype(BF16),
              final_norm_g.reshape(1, d))

    zeros_state = jnp.zeros((bp, RET_HEADS, HEAD_DIM, HEAD_DIM), F32)
    y_p, s_p, k_p, v_p = _run_group(x_prompt, mod[:bp], jnp.arange(seq_p), zeros_state, None, None, params)
    y_s, s_s, k_s, v_s = _run_group(x_sample, mod[bp:], past_len + jnp.arange(seq_s), state_ret[0],
                                    cache_sb_k, cache_sb_v, params)
    return (y_p, y_s, s_p, k_p, v_p, s_s, k_s, v_s)
```

```python
import functools

import jax
import jax.numpy as jnp
from jax import lax
from jax.experimental import pallas as pl
from jax.experimental.pallas import tpu as pltpu

F32 = jnp.float32
BF16 = jnp.bfloat16

NORM_EPS = 1e-6
ROPE_BASE = 10000.0
HEAD_DIM = 128
RET_HEADS = 4
SB_HEADS = 4
MIX_W = RET_HEADS * HEAD_DIM
PEER_HEADS = 8
PEER_TOPK = 16
SB_QBLOCK = 512
RET_CHUNK = 512
VMEM_LIMIT = 56 * 1024 * 1024
NEG_INF = float("-inf")


def _cparams(*sem):
    return pltpu.CompilerParams(dimension_semantics=sem, vmem_limit_bytes=VMEM_LIMIT)


def _dot(a, b):
    return jnp.dot(a, b, preferred_element_type=F32)


def _dot_nt(a, b):
    return lax.dot_general(a, b, (((1,), (1,)), ((), ())), preferred_element_type=F32)


def _split_bf16(x):
    hi = x.astype(BF16)
    lo = (x - hi.astype(F32)).astype(BF16)
    return hi, lo


def _rms(x):
    return x * lax.rsqrt(jnp.mean(x * x, axis=-1, keepdims=True) + NORM_EPS)


def _silu(x):
    return x / (1.0 + jnp.exp(-x))


def _adaln_kernel(c_ref, w_ref, b_ref, o_ref):
    s = _silu(c_ref[...])
    s_hi, s_lo = _split_bf16(s)
    w_hi, w_lo = _split_bf16(w_ref[...])
    o_ref[...] = _dot(s_hi, w_hi) + _dot(s_hi, w_lo) + _dot(s_lo, w_hi) + b_ref[...]


def _adaln(c, w_ada, b_ada):
    nb, d = c.shape
    n = w_ada.shape[1]
    tn = 1024
    return pl.pallas_call(
        _adaln_kernel,
        out_shape=jax.ShapeDtypeStruct((nb, n), F32),
        grid=(n // tn,),
        in_specs=[pl.BlockSpec((nb, d), lambda j: (0, 0)),
                  pl.BlockSpec((d, tn), lambda j: (0, j)),
                  pl.BlockSpec((1, tn), lambda j: (0, j))],
        out_specs=pl.BlockSpec((nb, tn), lambda j: (0, j)),
        compiler_params=_cparams("arbitrary"),
        name="adaln",
    )(c, w_ada, b_ada.reshape(1, n))


def _mod_blockspec(tm, rows_per_batch, d, ngrid):
    groups = max(1, tm // rows_per_batch)
    if groups == 1:
        per = rows_per_batch // tm
        if ngrid == 1:
            return groups, pl.BlockSpec((1, 6, d), lambda i: (i // per, 0, 0))
        return groups, pl.BlockSpec((1, 6, d), lambda i, k: (i // per, 0, 0))
    if ngrid == 1:
        return groups, pl.BlockSpec((groups, 6, d), lambda i: (i, 0, 0))
    return groups, pl.BlockSpec((groups, 6, d), lambda i, k: (i, 0, 0))


def _inproj_kernel(x_ref, mod_ref, g1_ref, w_ref, cos_ref, sin_ref,
                   rq_ref, rk_ref, rv_ref, rg_ref, sq_ref, sk_ref, sv_ref, skb_ref, svb_ref, a_scr, *, groups, rows):
    for g in range(groups):
        r = slice(g * rows, (g + 1) * rows)
        xn = _rms(x_ref[r, :]) * g1_ref[...]
        a = xn * (1.0 + mod_ref[g, 1:2, :]) + mod_ref[g, 0:1, :]
        a_scr[r, :] = a.astype(BF16)
    y = _dot(a_scr[...], w_ref[...])
    cos = cos_ref[...]
    sin = sin_ref[...]
    for h in range(RET_HEADS):
        c = slice(h * HEAD_DIM, (h + 1) * HEAD_DIM)
        yq = y[:, h * HEAD_DIM:(h + 1) * HEAD_DIM]
        yk = y[:, MIX_W + h * HEAD_DIM:MIX_W + (h + 1) * HEAD_DIM]
        rq = (yq * cos + pltpu.roll(yq, HEAD_DIM // 2, 1) * sin) * (HEAD_DIM ** -0.5)
        rk = yk * cos + pltpu.roll(yk, HEAD_DIM // 2, 1) * sin
        rq_ref[:, c] = rq.astype(rq_ref.dtype)
        rk_ref[:, c] = rk.astype(rk_ref.dtype)
    rv_ref[...] = y[:, 2 * MIX_W:3 * MIX_W].astype(rv_ref.dtype)
    rg_ref[...] = y[:, 3 * MIX_W:4 * MIX_W].astype(rg_ref.dtype)
    sq_ref[...] = y[:, 4 * MIX_W:5 * MIX_W].astype(sq_ref.dtype)
    sk_ref[...] = y[:, 5 * MIX_W:6 * MIX_W]
    sv_ref[...] = y[:, 6 * MIX_W:7 * MIX_W]
    skb_ref[...] = y[:, 5 * MIX_W:6 * MIX_W].astype(BF16)
    svb_ref[...] = y[:, 6 * MIX_W:7 * MIX_W].astype(BF16)


def _inproj(x, mod, g1, w_in, cos_t, sin_t, rows_per_batch, tm):
    n, d = x.shape
    ncol = w_in.shape[1]
    groups, mod_spec = _mod_blockspec(tm, rows_per_batch, d, 1)
    npos = cos_t.shape[0] // tm
    tok = lambda dt: jax.ShapeDtypeStruct((n, MIX_W), dt)
    tspec = pl.BlockSpec((tm, MIX_W), lambda i: (i, 0))
    return pl.pallas_call(
        functools.partial(_inproj_kernel, groups=groups, rows=tm // groups),
        out_shape=(tok(BF16), tok(BF16), tok(BF16), tok(BF16), tok(BF16), tok(F32), tok(F32), tok(BF16), tok(BF16)),
        grid=(n // tm,),
        in_specs=[pl.BlockSpec((tm, d), lambda i: (i, 0)),
                  mod_spec,
                  pl.BlockSpec((1, d), lambda i: (0, 0)),
                  pl.BlockSpec((d, ncol), lambda i: (0, 0)),
                  pl.BlockSpec((tm, HEAD_DIM), lambda i: (i % npos, 0)),
                  pl.BlockSpec((tm, HEAD_DIM), lambda i: (i % npos, 0))],
        out_specs=(tspec,) * 9,
        scratch_shapes=[pltpu.VMEM((tm, d), BF16)],
        compiler_params=_cparams("arbitrary"),
        name="inproj",
    )(x, mod, g1, w_in, cos_t, sin_t)


def _retention_kernel(q_ref, k_ref, v_ref, g_ref, s0_ref, intra_ref, qd_ref, kd_ref, cd_ref, gn_ref,
                      o_ref, sout_ref, s_scr):
    c = pl.program_id(1)

    @pl.when(c == 0)
    def _():
        s_scr[...] = s0_ref[0]

    for h in range(RET_HEADS):
        cs = slice(h * HEAD_DIM, (h + 1) * HEAD_DIM)
        q = q_ref[:, cs]
        k = k_ref[:, cs]
        v = v_ref[:, cs]
        state = s_scr[h]
        scores = _dot_nt(q, k) * intra_ref[h]
        o = _dot(scores.astype(BF16), v) + _dot(q, state.astype(BF16)) * qd_ref[h]
        kdt = (k.astype(F32) * kd_ref[h]).T.astype(BF16)
        s_scr[h] = cd_ref[h] * state + _dot(kdt, v)
        cen = o - jnp.mean(o, axis=-1, keepdims=True)
        on = cen * lax.rsqrt(jnp.mean(cen * cen, axis=-1, keepdims=True) + NORM_EPS)
        gate = _silu(g_ref[:, cs].astype(F32))
        o_ref[:, cs] = (on * gn_ref[:, cs] * gate).astype(o_ref.dtype)

    @pl.when(c == pl.num_programs(1) - 1)
    def _():
        sout_ref[0] = s_scr[...]


def _retention_tables(chunk):
    log_g = jnp.log1p(-jnp.exp2(-5.0 - jnp.arange(RET_HEADS, dtype=F32)))
    idx = jnp.arange(chunk, dtype=F32)
    diff = idx[:, None] - idx[None, :]
    causal = diff >= 0
    intra = jnp.where(causal[None], jnp.exp(jnp.where(causal, diff, 0.0)[None] * log_g[:, None, None]), 0.0)
    qd = jnp.exp((idx[None, :] + 1.0) * log_g[:, None])
    kd = jnp.exp((chunk - 1.0 - idx)[None, :] * log_g[:, None])
    cd = jnp.exp(chunk * log_g)
    bc = lambda t: jnp.broadcast_to(t[:, :, None], (RET_HEADS, chunk, HEAD_DIM))
    return intra, bc(qd), bc(kd), jnp.broadcast_to(cd[:, None, None], (RET_HEADS, 1, HEAD_DIM))


def _retention(rq, rk, rv, rg, s0, ret_norm_g, nbatch, seq, chunk):
    n = rq.shape[0]
    nc = seq // chunk
    intra, qd, kd, cd = _retention_tables(chunk)
    tspec = pl.BlockSpec((chunk, MIX_W), lambda b, c: (b * nc + c, 0))
    full = lambda shape: pl.BlockSpec(shape, lambda b, c: (0,) * len(shape))
    sspec = pl.BlockSpec((1, RET_HEADS, HEAD_DIM, HEAD_DIM), lambda b, c: (b, 0, 0, 0))
    return pl.pallas_call(
        _retention_kernel,
        out_shape=(jax.ShapeDtypeStruct((n, MIX_W), BF16),
                   jax.ShapeDtypeStruct((nbatch, RET_HEADS, HEAD_DIM, HEAD_DIM), F32)),
        grid=(nbatch, nc),
        in_specs=[tspec, tspec, tspec, tspec, sspec,
                  full((RET_HEADS, chunk, chunk)), full((RET_HEADS, chunk, HEAD_DIM)),
                  full((RET_HEADS, chunk, HEAD_DIM)), full((RET_HEADS, 1, HEAD_DIM)), full((1, MIX_W))],
        out_specs=(tspec, sspec),
        scratch_shapes=[pltpu.VMEM((RET_HEADS, HEAD_DIM, HEAD_DIM), F32)],
        compiler_params=_cparams("arbitrary", "arbitrary"),
        name="retention",
    )(rq, rk, rv, rg, s0, intra, qd, kd, cd, ret_norm_g)


def _log_sigmoid(z):
    return jnp.minimum(z, 0.0) - jnp.log(1.0 + jnp.exp(-jnp.abs(z)))


SB_TILE = 512
SB_CACHE_TILE = 2048


def _sb_scores(q, kblk):
    return _dot_nt(q, kblk) * (HEAD_DIM ** -0.5)


def _sb_cumulate(z, cumw, mask):
    nsub = z.shape[1] // 128
    log_beta = _log_sigmoid(z)
    log_keep = log_beta - z
    if mask is not None:
        log_keep = jnp.where(mask, log_keep, 0.0)
    parts = _split_bf16(log_keep)
    lhs = jnp.concatenate(
        [jnp.concatenate([p[:, c * 128:(c + 1) * 128] for p in parts], axis=1) for c in range(nsub)], axis=0)
    return log_beta, _dot(lhs, cumw)


def _sb_weights(log_beta, cw, vblk, carry, mask):
    nsub = vblk.shape[0] // 128
    qb = cw.shape[0] // nsub
    a = [None] * nsub
    for c in reversed(range(nsub)):
        a[c] = jnp.exp(log_beta[:, c * 128:(c + 1) * 128] + (cw[c * qb:(c + 1) * qb, 0:128] + carry))
        carry = carry + cw[c * qb:(c + 1) * qb, 128:256]
    a = jnp.concatenate(a, axis=1)
    if mask is not None:
        a = jnp.where(mask, a, 0.0)
    return _dot(a.astype(BF16), vblk), carry


def _sb_tiles(q_ref, k_of, v_of, cumw_ref, carry_of, mask):
    heads = _sb_heads()
    z, cw, res = {}, {}, {}
    for step in range(len(heads) + 2):
        if step < len(heads):
            z[step] = _sb_scores(q_ref[:, heads[step]], k_of(heads[step]))
        if 0 <= step - 1 < len(heads):
            cw[step - 1] = _sb_cumulate(z.pop(step - 1), cumw_ref[...], mask)
        if 0 <= step - 2 < len(heads):
            hs = heads[step - 2]
            res[step - 2] = _sb_weights(*cw.pop(step - 2), v_of(hs), carry_of(hs), mask)
    return [res[h] for h in range(len(heads))]


def _sb_heads():
    return [slice(h * HEAD_DIM, (h + 1) * HEAD_DIM) for h in range(SB_HEADS)]


def _sb_past_tiles(q_ref, kp_ref, vp_ref, cumw_ref, out_scr, carry_scr, ntiles):
    def body(it, _):
        start = pl.multiple_of((ntiles - 1 - it) * SB_TILE, SB_TILE)
        res = _sb_tiles(q_ref, lambda hs: kp_ref[0, pl.ds(start, SB_TILE), hs].astype(BF16),
                        lambda hs: vp_ref[0, pl.ds(start, SB_TILE), hs].astype(BF16),
                        cumw_ref, lambda hs: carry_scr[:, hs], None)
        for hs, (o, carry) in zip(_sb_heads(), res):
            out_scr[:, hs] += o
            carry_scr[:, hs] = carry
        return 0

    lax.fori_loop(0, ntiles, body, 0)


def _sb_prompt_kernel(q_ref, kp_ref, vp_ref, cumw_ref, o_ref, out_scr, carry_scr):
    i = pl.program_id(1)
    qb = q_ref.shape[0]
    last = (i * qb) // SB_TILE
    start = pl.multiple_of(last * SB_TILE, SB_TILE)
    q_pos = i * qb + lax.broadcasted_iota(jnp.int32, (qb, SB_TILE), 0)
    k_pos = start + lax.broadcasted_iota(jnp.int32, (qb, SB_TILE), 1)
    res = _sb_tiles(q_ref, lambda hs: kp_ref[0, pl.ds(start, SB_TILE), hs].astype(BF16),
                    lambda hs: vp_ref[0, pl.ds(start, SB_TILE), hs].astype(BF16),
                    cumw_ref, lambda hs: jnp.zeros((qb, 128), F32), k_pos < q_pos)
    for hs, (o, carry) in zip(_sb_heads(), res):
        out_scr[:, hs] = o
        carry_scr[:, hs] = carry
    _sb_past_tiles(q_ref, kp_ref, vp_ref, cumw_ref, out_scr, carry_scr, last)
    o_ref[...] = out_scr[...].astype(o_ref.dtype)


def _sb_sample_kernel(q_ref, kn_ref, vn_ref, kc_hbm, vc_hbm, cumw_ref, o_ref,
                      kpad, vpad, out_scr, carry_scr, kbuf, vbuf, sem):
    b = pl.program_id(0)
    qb = q_ref.shape[0]
    tile = kbuf.shape[2]
    ntiles = kc_hbm.shape[2] // tile
    head_of = {hs.start: h for h, hs in enumerate(_sb_heads())}

    def tile_copies(it, slot):
        start = pl.multiple_of((ntiles - 1 - it) * tile, tile)
        cps = []
        for h in range(SB_HEADS):
            cps.append(pltpu.make_async_copy(kc_hbm.at[0, b, pl.ds(start, tile), h, :],
                                             kbuf.at[slot, h], sem.at[0, slot, h]))
            cps.append(pltpu.make_async_copy(vc_hbm.at[0, b, pl.ds(start, tile), h, :],
                                             vbuf.at[slot, h], sem.at[1, slot, h]))
        return cps

    for cp in tile_copies(0, 0):
        cp.start()

    kpad[...] = jnp.zeros_like(kpad)
    vpad[...] = jnp.zeros_like(vpad)
    kpad[0:qb, :] = kn_ref[...]
    vpad[0:qb, :] = vn_ref[...]
    tq = lax.broadcasted_iota(jnp.int32, (qb, 128), 0)
    ts = lax.broadcasted_iota(jnp.int32, (qb, 128), 1)
    res = _sb_tiles(q_ref, lambda hs: kpad[:, hs].astype(BF16), lambda hs: vpad[:, hs].astype(BF16),
                    cumw_ref, lambda hs: jnp.zeros((qb, 128), F32), ts < tq)
    for hs, (o, carry) in zip(_sb_heads(), res):
        out_scr[:, hs] = o
        carry_scr[:, hs] = carry

    def body(it, _):
        slot = it % 2

        @pl.when(it + 1 < ntiles)
        def _():
            for cp in tile_copies(it + 1, 1 - slot):
                cp.start()

        for cp in tile_copies(it, slot):
            cp.wait()
        res = _sb_tiles(q_ref, lambda hs: kbuf[slot, head_of[hs.start]].astype(BF16),
                        lambda hs: vbuf[slot, head_of[hs.start]].astype(BF16),
                        cumw_ref, lambda hs: carry_scr[:, hs], None)
        for hs, (o, carry) in zip(_sb_heads(), res):
            out_scr[:, hs] += o
            carry_scr[:, hs] = carry
        return 0

    lax.fori_loop(0, ntiles, body, 0)
    o_ref[...] = out_scr[...].astype(o_ref.dtype)


def _cumw():
    n = 128
    later = lax.broadcasted_iota(jnp.int32, (n, n), 0) > lax.broadcasted_iota(jnp.int32, (n, n), 1)
    half = jnp.concatenate([later.astype(F32), jnp.ones((n, n), F32)], axis=1)
    return jnp.concatenate([half, half], axis=0).astype(BF16)


def _stick_breaking_prompt(sq, sk, sv, nbatch, seq):
    n = sq.shape[0]
    qb = SB_QBLOCK
    nq = seq // qb
    assert seq % SB_TILE == 0
    tspec = pl.BlockSpec((qb, MIX_W), lambda b, i: (b * nq + i, 0))
    pspec = pl.BlockSpec((1, seq, MIX_W), lambda b, i: (b, 0, 0))
    return pl.pallas_call(
        _sb_prompt_kernel,
        out_shape=jax.ShapeDtypeStruct((n, MIX_W), BF16),
        grid=(nbatch, nq),
        in_specs=[tspec, pspec, pspec, pl.BlockSpec((256, 256), lambda b, i: (0, 0))],
        out_specs=tspec,
        scratch_shapes=[pltpu.VMEM((qb, MIX_W), F32), pltpu.VMEM((qb, MIX_W), F32)],
        compiler_params=_cparams("arbitrary", "arbitrary"),
        name="stick_breaking",
    )(sq, sk.reshape(nbatch, seq, MIX_W), sv.reshape(nbatch, seq, MIX_W), _cumw())


def _stick_breaking_sample(sq, sk, sv, k_cache, v_cache, nbatch, seq):
    n = sq.shape[0]
    past_len = k_cache.shape[2]
    tile = SB_CACHE_TILE if past_len % SB_CACHE_TILE == 0 else SB_TILE
    assert seq <= 128 and past_len % tile == 0
    tspec = pl.BlockSpec((seq, MIX_W), lambda b: (b, 0))
    hbm = pl.BlockSpec(memory_space=pl.ANY)
    return pl.pallas_call(
        _sb_sample_kernel,
        out_shape=jax.ShapeDtypeStruct((n, MIX_W), BF16),
        grid=(nbatch,),
        in_specs=[tspec, tspec, tspec, hbm, hbm, pl.BlockSpec((256, 256), lambda b: (0, 0))],
        out_specs=tspec,
        scratch_shapes=[pltpu.VMEM((128, MIX_W), F32), pltpu.VMEM((128, MIX_W), F32),
                        pltpu.VMEM((seq, MIX_W), F32), pltpu.VMEM((seq, MIX_W), F32),
                        pltpu.VMEM((2, SB_HEADS, tile, HEAD_DIM), F32),
                        pltpu.VMEM((2, SB_HEADS, tile, HEAD_DIM), F32),
                        pltpu.SemaphoreType.DMA((2, 2, SB_HEADS))],
        compiler_params=_cparams("arbitrary"),
        name="stick_breaking_cached",
    )(sq, sk, sv, k_cache, v_cache, _cumw())


def _post_kernel(x_ref, ret_ref, so_ref, mod_ref, wo_ref, g2_ref, wqt_ref, keys_ref,
                 h1_ref, bt_ref, st_ref, b_scr, *, groups, rows):
    attn = _dot(ret_ref[...], wo_ref[0:MIX_W, :]) + _dot(so_ref[...], wo_ref[MIX_W:2 * MIX_W, :])
    for g in range(groups):
        r = slice(g * rows, (g + 1) * rows)
        h1 = x_ref[r, :] + mod_ref[g, 2:3, :] * attn[r, :]
        h1_ref[r, :] = h1
        b_scr[r, :] = _rms(h1) * g2_ref[...] * (1.0 + mod_ref[g, 4:5, :]) + mod_ref[g, 3:4, :]
    bt = b_scr[...].T.astype(BF16)
    bt_ref[...] = bt
    qt = _dot(wqt_ref[...], bt).astype(BF16)
    for hp in range(2 * PEER_HEADS):
        rs = slice(hp * HEAD_DIM, (hp + 1) * HEAD_DIM)
        st_ref[rs, :] = _dot(keys_ref[hp % 2, hp // 2], qt[rs, :])


def _post(x, ret, so, mod, w_out, g2, wq_t, keys, rows_per_batch, tm):
    n, d = x.shape
    nq = wq_t.shape[0]
    groups, mod_spec = _mod_blockspec(tm, rows_per_batch, d, 1)
    return pl.pallas_call(
        functools.partial(_post_kernel, groups=groups, rows=tm // groups),
        out_shape=(jax.ShapeDtypeStruct((n, d), F32),
                   jax.ShapeDtypeStruct((d, n), BF16),
                   jax.ShapeDtypeStruct((nq, n), F32)),
        grid=(n // tm,),
        in_specs=[pl.BlockSpec((tm, d), lambda i: (i, 0)),
                  pl.BlockSpec((tm, MIX_W), lambda i: (i, 0)),
                  pl.BlockSpec((tm, MIX_W), lambda i: (i, 0)),
                  mod_spec,
                  pl.BlockSpec((2 * MIX_W, d), lambda i: (0, 0)),
                  pl.BlockSpec((1, d), lambda i: (0, 0)),
                  pl.BlockSpec((nq, d), lambda i: (0, 0)),
                  pl.BlockSpec(keys.shape, lambda i: (0, 0, 0, 0))],
        out_specs=(pl.BlockSpec((tm, d), lambda i: (i, 0)),
                   pl.BlockSpec((d, tm), lambda i: (0, i)),
                   pl.BlockSpec((nq, tm), lambda i: (0, i))),
        scratch_shapes=[pltpu.VMEM((tm, d), F32)],
        compiler_params=_cparams("arbitrary"),
        name="post_mix",
    )(x, ret, so, mod, w_out, g2, wq_t, keys)


def _stack_rows(rows):
    t = rows[0].shape[1]
    ridx = lax.broadcasted_iota(jnp.int32, (8, t), 0)
    out = jnp.broadcast_to(rows[0], (8, t))
    for r in range(1, len(rows)):
        out = jnp.where(ridx == r, rows[r], out)
    return out


def _top_values(x, count):
    vals = []
    for _ in range(count):
        m = jnp.max(x, axis=0, keepdims=True)
        vals.append(m)
        x = jnp.where(x == m, NEG_INF, x)
    return vals


def _bitonic_pairs(n):
    pairs = []
    k = 2
    while k <= n:
        j = k // 2
        while j >= 1:
            for i in range(n):
                l = i ^ j
                if l > i:
                    pairs.append((i, l, (i & k) == 0))
            j //= 2
        k *= 2
    return pairs


def _top16_of_128(x):
    c = [x[8 * v:8 * (v + 1), :] for v in range(16)]
    for i, j, desc in _bitonic_pairs(16):
        hi, lo = jnp.maximum(c[i], c[j]), jnp.minimum(c[i], c[j])
        c[i], c[j] = (hi, lo) if desc else (lo, hi)
    vals = []
    for r in range(PEER_TOPK):
        m = jnp.max(c[0], axis=0, keepdims=True)
        vals.append(m)
        if r + 1 < PEER_TOPK:
            hit = c[0] == m
            depth = PEER_TOPK - r
            for k in range(depth - 1):
                c[k] = jnp.where(hit, c[k + 1], c[k])
            c[depth - 1] = jnp.where(hit, NEG_INF, c[depth - 1])
    return vals


def _count_prefix(test, rows):
    t1 = test(rows[7])
    t2 = test(jnp.where(t1, rows[11], rows[3]))
    lo3 = jnp.where(t2, rows[5], rows[1])
    hi3 = jnp.where(t2, rows[13], rows[9])
    t3 = test(jnp.where(t1, hi3, lo3))
    c00 = jnp.where(t3, rows[2], rows[0])
    c01 = jnp.where(t3, rows[6], rows[4])
    c10 = jnp.where(t3, rows[10], rows[8])
    c11 = jnp.where(t3, rows[14], rows[12])
    t4 = test(jnp.where(t1, jnp.where(t2, c11, c10), jnp.where(t2, c01, c00)))
    t5 = test(rows[15])
    count = jnp.where(t1, 8.0, 0.0) + jnp.where(t2, 4.0, 0.0) + jnp.where(t3, 2.0, 0.0)
    return count + jnp.where(t4, 1.0, 0.0) + jnp.where(t5, 1.0, 0.0)


def _topk_kernel(s_ref, l1_ref, e1_ref, r2_ref, e2_ref):
    tt = s_ref.shape[1]
    ridx = lax.broadcasted_iota(jnp.int32, (8, tt), 0)
    for h in range(PEER_HEADS):
        base = 2 * h * HEAD_DIM
        s1 = s_ref[base:base + HEAD_DIM, :]
        s2 = s_ref[base + HEAD_DIM:base + 2 * HEAD_DIM, :]
        a = _top16_of_128(s1)
        b = _top16_of_128(s2)
        a_lo, a_hi = _stack_rows(a[0:8]), _stack_rows(a[8:16])
        b_lo, b_hi = _stack_rows(b[0:8]), _stack_rows(b[8:16])
        b_mid = jnp.where(ridx >= 5, b_lo, NEG_INF)
        cand = jnp.concatenate(
            [a_lo + b[l] for l in range(5)] + [a_hi + b[0], b_hi + a[0], b_mid + a[0], b_mid + a[1]], axis=0)
        top = _top_values(cand, PEER_TOPK)
        tau = top[PEER_TOPK - 1]
        z = jnp.zeros_like(tau)
        for v in top:
            z = z + jnp.exp(v - top[0])
        l1_ref[h] = _count_prefix(lambda row: s1 + row >= tau, b)
        e1_ref[h] = jnp.exp(s1 - a[0]) / z
        r2_ref[h * HEAD_DIM:(h + 1) * HEAD_DIM, :] = _count_prefix(lambda row: row > s2, b)
        e2_ref[h * HEAD_DIM:(h + 1) * HEAD_DIM, :] = jnp.exp(s2 - b[0])


def _topk_stats(st, tt):
    nq, n = st.shape
    rows = PEER_HEADS * HEAD_DIM
    spec3 = pl.BlockSpec((PEER_HEADS, HEAD_DIM, tt), lambda i: (0, 0, i))
    spec2 = pl.BlockSpec((rows, tt), lambda i: (0, i))
    return pl.pallas_call(
        _topk_kernel,
        out_shape=(jax.ShapeDtypeStruct((PEER_HEADS, HEAD_DIM, n), F32),
                   jax.ShapeDtypeStruct((PEER_HEADS, HEAD_DIM, n), F32),
                   jax.ShapeDtypeStruct((rows, n), F32),
                   jax.ShapeDtypeStruct((rows, n), F32)),
        grid=(n // tt,),
        in_specs=[pl.BlockSpec((nq, tt), lambda i: (0, i))],
        out_specs=(spec3, spec3, spec2, spec2),
        compiler_params=_cparams("arbitrary"),
        name="peer_topk",
    )(st)


PEER_SUB = 1024
PEER_STEP = 2048
TOKEN_TILE = 512
TOPK_TILE = 256


def _gelu(x):
    half = 0.5 * x
    inner = x * (0.7978845608028654 + (0.7978845608028654 * 0.044715) * (x * x))
    return half + half * jnp.tanh(inner)


def _peer_kernel(bt_ref, u_ref, vt_ref, r2_ref, e2_ref, l1_ref, e1_ref, h1_ref, mod_ref, gf_ref, y_ref,
                 acc_ref, wa_scr, act_scr, r2_scr, e2_scr, *, groups, rows, ec):
    k = pl.program_id(1)
    t = bt_ref.shape[1]

    @pl.when(k == 0)
    def _():
        acc_ref[...] = jnp.zeros_like(acc_ref)
        r2_scr[...] = r2_ref[...].astype(BF16)
        e2_scr[...] = e2_ref[...].astype(BF16)

    def bcast_bf16(row):
        return jnp.tile(jnp.broadcast_to(row, (16, 128)).astype(BF16), (HEAD_DIM // 16, 1))

    nsub = PEER_SUB // HEAD_DIM
    igrp = min(nsub, 4)
    nchunk = ec // PEER_SUB

    def activations(sc):
        es = slice(sc * PEER_SUB, (sc + 1) * PEER_SUB)
        act_scr[es, :] = _gelu(_dot(u_ref[es, :], bt_ref[...])).astype(BF16)

    def gates(sc):
        for tl in range(t // 128):
            ls = slice(tl * 128, (tl + 1) * 128)
            for ig in range(nsub // igrp):
                w = [jnp.zeros((HEAD_DIM, 128), BF16) for _ in range(igrp)]
                for h in range(PEER_HEADS):
                    hs = slice(h * HEAD_DIM, (h + 1) * HEAD_DIM)
                    rank2 = r2_scr[hs, ls]
                    e2 = e2_scr[hs, ls]
                    for ii in range(igrp):
                        il = sc * nsub + ig * igrp + ii
                        limit1 = bcast_bf16(l1_ref[h, il:il + 1, ls])
                        e1 = bcast_bf16(e1_ref[h, il:il + 1, ls])
                        w[ii] = w[ii] + jnp.where(rank2 < limit1, e2 * e1, 0)
                for ii in range(igrp):
                    il = sc * nsub + ig * igrp + ii
                    rs = slice(il * HEAD_DIM, (il + 1) * HEAD_DIM)
                    wa_scr[rs, ls] = w[ii] * act_scr[rs, ls]

    def values(sc):
        es = slice(sc * PEER_SUB, (sc + 1) * PEER_SUB)
        acc_ref[...] += lax.dot_general(vt_ref[es, :], wa_scr[es, :], (((0,), (0,)), ((), ())),
                                        preferred_element_type=F32)

    activations(0)
    for sc in range(nchunk):
        if sc + 1 < nchunk:
            activations(sc + 1)
        gates(sc)
        values(sc)

    @pl.when(k == pl.num_programs(1) - 1)
    def _():
        out = acc_ref[...].T
        for g in range(groups):
            r = slice(g * rows, (g + 1) * rows)
            h2 = h1_ref[r, :] + mod_ref[g, 5:6, :] * out[r, :]
            y_ref[r, :] = _rms(h2) * gf_ref[...]


def _peer(bt, u, vt, l1, e1, r2, e2, h1, mod, gf, rows_per_batch, t, ec):
    d, n = bt.shape
    ne = u.shape[0]
    nr = r2.shape[0]
    groups, mod_spec = _mod_blockspec(t, rows_per_batch, d, 2)
    spec_i = pl.BlockSpec((PEER_HEADS, ec // HEAD_DIM, t), lambda i, k: (0, k, i))
    spec_j = pl.BlockSpec((nr, t), lambda i, k: (0, i))
    return pl.pallas_call(
        functools.partial(_peer_kernel, groups=groups, rows=t // groups, ec=ec),
        out_shape=jax.ShapeDtypeStruct((n, d), F32),
        grid=(n // t, ne // ec),
        in_specs=[pl.BlockSpec((d, t), lambda i, k: (0, i)),
                  pl.BlockSpec((ec, d), lambda i, k: (k, 0)),
                  pl.BlockSpec((ec, d), lambda i, k: (k, 0)),
                  spec_j, spec_j, spec_i, spec_i,
                  pl.BlockSpec((t, d), lambda i, k: (i, 0)),
                  mod_spec,
                  pl.BlockSpec((1, d), lambda i, k: (0, 0))],
        out_specs=pl.BlockSpec((t, d), lambda i, k: (i, 0)),
        scratch_shapes=[pltpu.VMEM((d, t), F32), pltpu.VMEM((ec, t), BF16), pltpu.VMEM((ec, t), BF16),
                        pltpu.VMEM((nr, t), BF16), pltpu.VMEM((nr, t), BF16)],
        compiler_params=_cparams("arbitrary", "arbitrary"),
        name="peer_dense",
    )(bt, u, vt, r2, e2, l1, e1, h1, mod, gf)


def _rope_tables(pos):
    half = HEAD_DIM // 2
    inv_freq = ROPE_BASE ** (-jnp.arange(half, dtype=F32) / half)
    ang = pos.astype(F32)[:, None] * inv_freq[None, :]
    cos, sin = jnp.cos(ang), jnp.sin(ang)
    return jnp.concatenate([cos, cos], axis=1), jnp.concatenate([-sin, sin], axis=1)


def _tile(n, pref):
    t = min(n, pref)
    assert n % t == 0
    return t


def _run_group(x, mod, pos, s0, k_past, v_past, params):
    (g1, w_in, ret_g, w_out, g2, wq_t, keys, u, vt, gf) = params
    nbatch, seq, d = x.shape
    n = nbatch * seq
    x2 = x.reshape(n, d)

    tm = _tile(n, TOKEN_TILE)
    reps = max(1, tm // seq)
    cos_t, sin_t = _rope_tables(pos)
    cos_t, sin_t = jnp.tile(cos_t, (reps, 1)), jnp.tile(sin_t, (reps, 1))
    rq, rk, rv, rg, sq, sk, sv, skb, svb = _inproj(x2, mod, g1, w_in, cos_t, sin_t, seq, tm)

    chunk = min(seq, RET_CHUNK)
    ret, s_new = _retention(rq, rk, rv, rg, s0, ret_g, nbatch, seq, chunk)

    if k_past is None:
        so = _stick_breaking_prompt(sq, skb, svb, nbatch, seq)
    else:
        so = _stick_breaking_sample(sq, sk, sv, k_past, v_past, nbatch, seq)

    tp = _tile(n, TOKEN_TILE)
    h1, bt, st = _post(x2, ret, so, mod, w_out, g2, wq_t, keys, seq, tp)
    l1, e1, r2, e2 = _topk_stats(st, _tile(n, TOPK_TILE))
    y = _peer(bt, u, vt, l1, e1, r2, e2, h1, mod, gf, seq, tp, PEER_STEP)
    return (y.reshape(nbatch, seq, d),
            s_new.reshape(1, nbatch, RET_HEADS, HEAD_DIM, HEAD_DIM),
            sk.reshape(1, nbatch, seq, SB_HEADS, HEAD_DIM),
            sv.reshape(1, nbatch, seq, SB_HEADS, HEAD_DIM))


def kernel(x_prompt, x_sample, c_prompt, c_sample, state_ret, cache_sb_k, cache_sb_v, w_ada, b_ada, norm1_g,
           w_in, ret_norm_g, w_out, norm2_g, peer_w_query, peer_sub_keys, peer_u, peer_v, final_norm_g):
    assert w_ada.shape[0] == 1, "single-layer step"
    bp, seq_p, d = x_prompt.shape
    bs, seq_s, _ = x_sample.shape
    past_len = cache_sb_k.shape[2]

    mod = _adaln(jnp.concatenate([c_prompt, c_sample], axis=0), w_ada[0], b_ada[0]).reshape(bp + bs, 6, d)
    params = (norm1_g[0].reshape(1, d), w_in[0].astype(BF16), ret_norm_g[0].reshape(1, MIX_W),
              w_out[0].astype(BF16), norm2_g[0].reshape(1, d), peer_w_query[0].T.astype(BF16),
              peer_sub_keys[0].astype(BF16), peer_u[0].astype(BF16), peer_v[0].astype(BF16),
              final_norm_g.reshape(1, d))

    zeros_state = jnp.zeros((bp, RET_HEADS, HEAD_DIM, HEAD_DIM), F32)
    y_p, s_p, k_p, v_p = _run_group(x_prompt, mod[:bp], jnp.arange(seq_p), zeros_state, None, None, params)
    y_s, s_s, k_s, v_s = _run_group(x_sample, mod[bp:], past_len + jnp.arange(seq_s), state_ret[0],
                                    cache_sb_k, cache_sb_v, params)
    return (y_p, y_s, s_p, k_p, v_p, s_s, k_s, v_s)
```
